```python
import math
import jax, jax.numpy as jnp
from jax import lax
import numpy as np

D_MODEL = 1024
BATCH = 16
SEQ = 2048
DEPTH = 4

D_MIX = 2 * D_MODEL
S5_WIDTH = D_MIX // 4
S5_GROUP = 16
S5_GROUPS = S5_WIDTH // S5_GROUP
S5_STATE = 64
S5_DT_MIN = 1e-3
S5_DT_MAX = 1e-1
SSD_HEADDIM = 64
SSD_WIDTH = 3 * D_MIX // 8
SSD_HEADS = SSD_WIDTH // SSD_HEADDIM
SSD_GROUPS = 2
SSD_STATE = 128
SSD_CONV = 4
SSD_CHUNK = 128
SSD_XBC = SSD_WIDTH + 2 * SSD_GROUPS * SSD_STATE
MLSTM_WIDTH = D_MIX - S5_WIDTH - SSD_WIDTH
MLSTM_HEADS = 6
MLSTM_VDIM = MLSTM_WIDTH // MLSTM_HEADS
MLSTM_QKDIM = MLSTM_VDIM // 2
MLSTM_QK = MLSTM_HEADS * MLSTM_QKDIM
MLSTM_CONV = 4
MLSTM_CHUNK = 64
D_FF = ((8 * D_MODEL // 3 + 255) // 256) * 256
NORM_EPS = 1e-6
IN_SPLIT_SIZES = (S5_WIDTH, SSD_WIDTH, SSD_XBC, SSD_HEADS, 2 * MLSTM_QK, MLSTM_WIDTH,
                  MLSTM_HEADS, MLSTM_HEADS, MLSTM_WIDTH)
D_IN = sum(IN_SPLIT_SIZES)

kernel_name = "hybrid_s5_ssd_mlstm_parallel_heads"


def _f32(a):
    return a.astype(jnp.float32)


def rmsnorm(x, g):
    xf = _f32(x)
    y = xf * lax.rsqrt(jnp.mean(xf * xf, axis=-1, keepdims=True) + NORM_EPS) * _f32(g)
    return y.astype(x.dtype)


def causal_conv(x, w, b):
    k = w.shape[0]
    y = lax.conv_general_dilated(x, w[:, None, :], window_strides=(1,), padding=[(k - 1, 0)],
                                 dimension_numbers=('NWC', 'WIO', 'NWC'),
                                 feature_group_count=x.shape[-1])
    return y + b


def s5_mixer(u, lam_re, lam_im, log_step, b_re, b_im, c_re, c_im, d_skip, w_glu, b_glu):
    bsz, seq, _ = u.shape
    ug = u.reshape(bsz, seq, S5_GROUPS, S5_GROUP)
    step = jnp.exp(log_step)[:, None]
    mag = jnp.exp(lam_re * step)
    ang = lam_im * step
    a_re = mag * jnp.cos(ang)
    a_im = mag * jnp.sin(ang)
    den = lam_re * lam_re + lam_im * lam_im
    nr = a_re - 1.0
    f_re = (nr * lam_re + a_im * lam_im) / den
    f_im = (a_im * lam_re - nr * lam_im) / den
    bb_re = f_re[..., None] * b_re - f_im[..., None] * b_im
    bb_im = f_re[..., None] * b_im + f_im[..., None] * b_re
    bu_re = jnp.einsum('blgh,gph->lbgp', ug, bb_re)
    bu_im = jnp.einsum('blgh,gph->lbgp', ug, bb_im)
    shp = (seq, 1, S5_GROUPS, S5_STATE)
    ar_t = jnp.broadcast_to(a_re[None, None], shp)
    ai_t = jnp.broadcast_to(a_im[None, None], shp)

    def combine(left, right):
        ar_l, ai_l, br_l, bi_l = left
        ar_r, ai_r, br_r, bi_r = right
        ar = ar_r * ar_l - ai_r * ai_l
        ai = ar_r * ai_l + ai_r * ar_l
        br = ar_r * br_l - ai_r * bi_l + br_r
        bi = ar_r * bi_l + ai_r * br_l + bi_r
        return ar, ai, br, bi

    _, _, s_re, s_im = lax.associative_scan(combine, (ar_t, ai_t, bu_re, bu_im), axis=0)
    y = jnp.einsum('lbgp,ghp->blgh', s_re, c_re) - jnp.einsum('lbgp,ghp->blgh', s_im, c_im)
    y = (y + d_skip * ug).reshape(bsz, seq, S5_WIDTH)
    g = jax.nn.gelu(y)
    return g * jax.nn.sigmoid(g @ w_glu + b_glu)


def ssd_mixer(z, xbc, dt_raw, conv_w, conv_b, dt_bias, a_log, d_skip, norm_g):
    bsz, seq, _ = z.shape
    hpg = SSD_HEADS // SSD_GROUPS
    q = SSD_CHUNK
    nc = seq // q
    xbc = jax.nn.silu(causal_conv(xbc, conv_w, conv_b))
    xs, bm, cm = jnp.split(xbc, [SSD_WIDTH, SSD_WIDTH + SSD_GROUPS * SSD_STATE], axis=-1)
    dt = jax.nn.softplus(dt_raw + dt_bias)
    a = -jnp.exp(a_log).reshape(SSD_GROUPS, hpg)
    x = xs.reshape(bsz, nc, q, SSD_GROUPS, hpg, SSD_HEADDIM)
    bm = bm.reshape(bsz, nc, q, SSD_GROUPS, SSD_STATE)
    cm = cm.reshape(bsz, nc, q, SSD_GROUPS, SSD_STATE)
    dtc = dt.reshape(bsz, nc, q, SSD_GROUPS, hpg)
    xdt = x * dtc[..., None]
    acs = jnp.cumsum(dtc * a, axis=2)
    causal = jnp.tril(jnp.ones((q, q), dtype=bool))
    seg = acs[:, :, :, None] - acs[:, :, None]
    decay = jnp.exp(jnp.where(causal[:, :, None, None], seg, -jnp.inf))
    cb = jnp.einsum('bclgn,bcsgn->bclsg', cm, bm)
    y_diag = jnp.einsum('bclsg,bclsgj,bcsgjp->bclgjp', cb, decay, xdt)
    decay_end = jnp.exp(acs[:, :, -1:] - acs)
    states = jnp.einsum('bcsgn,bcsgj,bcsgjp->bcgjpn', bm, decay_end, xdt)
    chunk_decay = jnp.exp(acs[:, :, -1])

    def step(h, inp):
        st, dec = inp
        return h * dec[..., None, None] + st, h

    h0 = jnp.zeros((bsz, SSD_GROUPS, hpg, SSD_HEADDIM, SSD_STATE), xdt.dtype)
    _, h_prev = lax.scan(step, h0, (jnp.moveaxis(states, 1, 0), jnp.moveaxis(chunk_decay, 1, 0)))
    h_prev = jnp.moveaxis(h_prev, 0, 1)
    y_off = jnp.einsum('bclgn,bcgjpn,bclgj->bclgjp', cm, h_prev, jnp.exp(acs))
    y = y_diag + y_off + x * d_skip.reshape(SSD_GROUPS, hpg)[..., None]
    y = y.reshape(bsz, seq, SSD_WIDTH) * jax.nn.silu(z)
    y = rmsnorm(y.reshape(bsz, seq, SSD_GROUPS, SSD_WIDTH // SSD_GROUPS),
                norm_g.reshape(SSD_GROUPS, SSD_WIDTH // SSD_GROUPS))
    return y.reshape(bsz, seq, SSD_WIDTH)


def mlstm_mixer(qk, v, i_raw, f_raw, o_raw, conv_w, conv_b, b_i, b_f):
    bsz, seq, _ = v.shape
    hh, dk, dv, t = MLSTM_HEADS, MLSTM_QKDIM, MLSTM_VDIM, MLSTM_CHUNK
    nc = seq // t
    qk = jax.nn.silu(causal_conv(qk, conv_w, conv_b))
    q, k = jnp.split(qk, [MLSTM_QK], axis=-1)
    q = q.reshape(bsz, nc, t, hh, dk)
    k = k.reshape(bsz, nc, t, hh, dk) * (dk ** -0.5)
    v = v.reshape(bsz, nc, t, hh, dv)
    log_i = (i_raw + b_i).reshape(bsz, nc, t, hh)
    log_f = jax.nn.log_sigmoid(f_raw + b_f).reshape(bsz, nc, t, hh)
    bcum = jnp.cumsum(log_f, axis=2)
    causal = jnp.tril(jnp.ones((t, t), dtype=bool))
    dmat = bcum[:, :, :, None, :] - bcum[:, :, None, :, :] + log_i[:, :, None, :, :]
    dmat = jnp.where(causal[:, :, None], dmat, -jnp.inf)
    w_end = bcum[:, :, -1:, :] - bcum + log_i
    a_loc = jnp.max(w_end, axis=2)
    e_end = jnp.exp(w_end - a_loc[:, :, None])
    c_loc = jnp.einsum('bcsh,bcshk,bcshv->bchkv', e_end, k, v)
    n_loc = jnp.einsum('bcsh,bcshk->bchk', e_end, k)
    g_chunk = bcum[:, :, -1]

    def step(carry, inp):
        c_st, n_st, m_st = carry
        c_l, n_l, a_l, g = inp
        m_new = jnp.maximum(g + m_st, a_l)
        s_old = jnp.exp(g + m_st - m_new)
        s_new = jnp.exp(a_l - m_new)
        c_new = s_old[..., None, None] * c_st + s_new[..., None, None] * c_l
        n_new = s_old[..., None] * n_st + s_new[..., None] * n_l
        return (c_new, n_new, m_new), (c_st, n_st, m_st)

    init = (jnp.zeros((bsz, hh, dk, dv), v.dtype), jnp.zeros((bsz, hh, dk), v.dtype),
            jnp.zeros((bsz, hh), v.dtype))
    xs = (jnp.moveaxis(c_loc, 1, 0), jnp.moveaxis(n_loc, 1, 0),
          jnp.moveaxis(a_loc, 1, 0), jnp.moveaxis(g_chunk, 1, 0))
    _, (c_prev, n_prev, m_prev) = lax.scan(step, init, xs)
    c_prev = jnp.moveaxis(c_prev, 0, 1)
    n_prev = jnp.moveaxis(n_prev, 0, 1)
    m_prev = jnp.moveaxis(m_prev, 0, 1)
    w_inter = bcum + m_prev[:, :, None, :]
    m_row = jnp.maximum(w_inter, jnp.max(dmat, axis=3))
    s_inter = jnp.exp(w_inter - m_row)
    p = jnp.exp(dmat - m_row[:, :, :, None, :])
    qk_s = jnp.einsum('bcthk,bcshk->bctsh', q, k) * p
    num = (jnp.einsum('bctsh,bcshv->bcthv', qk_s, v)
           + s_inter[..., None] * jnp.einsum('bcthk,bchkv->bcthv', q, c_prev))
    den = jnp.sum(qk_s, axis=3) + s_inter * jnp.einsum('bcthk,bchk->bcth', q, n_prev)
    den = jnp.maximum(jnp.abs(den), jnp.exp(-m_row))
    h = (num / den[..., None]).reshape(bsz, seq, MLSTM_WIDTH)
    return jax.nn.sigmoid(o_raw) * h


def hybrid_layer(x, norm_pre_mix, w_in, s5_lambda_re, s5_lambda_im, s5_log_step, s5_b_re, s5_b_im,
                 s5_c_re, s5_c_im, s5_d, s5_w_glu, s5_b_glu, ssd_conv_w, ssd_conv_b, ssd_dt_bias,
                 ssd_a_log, ssd_d, ssd_norm, mlstm_conv_w, mlstm_conv_b, mlstm_b_i, mlstm_b_f,
                 w_out, norm_post_mix, norm_pre_ffn, w_gate, w_up, w_down, norm_post_ffn):
    h = rmsnorm(x, norm_pre_mix)
    proj = _f32(h @ w_in)
    offs = list(np.cumsum(IN_SPLIT_SIZES)[:-1])
    u, z, xbc, dt_raw, qk, v, i_raw, f_raw, o_raw = jnp.split(proj, [int(o) for o in offs], axis=-1)
    y_a = s5_mixer(u, _f32(s5_lambda_re), _f32(s5_lambda_im), _f32(s5_log_step), _f32(s5_b_re),
                   _f32(s5_b_im), _f32(s5_c_re), _f32(s5_c_im), _f32(s5_d), _f32(s5_w_glu), _f32(s5_b_glu))
    y_b = ssd_mixer(z, xbc, dt_raw, _f32(ssd_conv_w), _f32(ssd_conv_b), _f32(ssd_dt_bias),
                    _f32(ssd_a_log), _f32(ssd_d), _f32(ssd_norm))
    y_c = mlstm_mixer(qk, v, i_raw, f_raw, o_raw, _f32(mlstm_conv_w), _f32(mlstm_conv_b),
                      _f32(mlstm_b_i), _f32(mlstm_b_f))
    mix = jnp.concatenate([y_a, y_b, y_c], axis=-1).astype(x.dtype) @ w_out
    x = x + rmsnorm(mix, norm_post_mix)
    h = rmsnorm(x, norm_pre_ffn)
    ff = (jax.nn.silu(h @ w_gate) * (h @ w_up)) @ w_down
    return x + rmsnorm(ff, norm_post_ffn)


def setup_inputs(seed: int = 0) -> dict:
    key = jax.random.key(seed)
    ks = iter(jax.random.split(key, 32))
    nrm = lambda shape, s: jax.random.normal(next(ks), shape, jnp.float32) * s
    gain = lambda shape: 1.0 + nrm(shape, 0.02)
    L_ = DEPTH
    n_idx = jnp.arange(S5_STATE, dtype=jnp.float32)
    log_dt_ssd = jax.random.uniform(next(ks), (L_, SSD_HEADS), jnp.float32,
                                    math.log(1e-3), math.log(1e-1))
    dt_ssd = jnp.exp(log_dt_ssd)
    return {
        "x": nrm((BATCH, SEQ, D_MODEL), 1.0),
        "norm_pre_mix": gain((L_, D_MODEL)),
        "w_in": nrm((L_, D_MODEL, D_IN), D_MODEL ** -0.5),
        "s5_lambda_re": -0.5 + nrm((L_, S5_GROUPS, S5_STATE), 0.01),
        "s5_lambda_im": jnp.pi * n_idx + nrm((L_, S5_GROUPS, S5_STATE), 0.01),
        "s5_log_step": jax.random.uniform(next(ks), (L_, S5_GROUPS), jnp.float32,
                                          math.log(S5_DT_MIN), math.log(S5_DT_MAX)),
        "s5_b_re": nrm((L_, S5_GROUPS, S5_STATE, S5_GROUP), (2 * S5_GROUP) ** -0.5),
        "s5_b_im": nrm((L_, S5_GROUPS, S5_STATE, S5_GROUP), (2 * S5_GROUP) ** -0.5),
        "s5_c_re": nrm((L_, S5_GROUPS, S5_GROUP, S5_STATE), (2 * S5_STATE) ** -0.5),
        "s5_c_im": nrm((L_, S5_GROUPS, S5_GROUP, S5_STATE), (2 * S5_STATE) ** -0.5),
        "s5_d": nrm((L_, S5_GROUPS, S5_GROUP), 1.0),
        "s5_w_glu": nrm((L_, S5_WIDTH, S5_WIDTH), S5_WIDTH ** -0.5),
        "s5_b_glu": nrm((L_, S5_WIDTH), 0.01),
        "ssd_conv_w": nrm((L_, SSD_CONV, SSD_XBC), SSD_CONV ** -0.5),
        "ssd_conv_b": nrm((L_, SSD_XBC), 0.01),
        "ssd_dt_bias": dt_ssd + jnp.log(-jnp.expm1(-dt_ssd)),
        "ssd_a_log": jnp.log(jax.random.uniform(next(ks), (L_, SSD_HEADS), jnp.float32, 1.0, 16.0)),
        "ssd_d": 1.0 + nrm((L_, SSD_HEADS), 0.1),
        "ssd_norm": gain((L_, SSD_WIDTH)),
        "mlstm_conv_w": nrm((L_, MLSTM_CONV, 2 * MLSTM_QK), MLSTM_CONV ** -0.5),
        "mlstm_conv_b": nrm((L_, 2 * MLSTM_QK), 0.01),
        "mlstm_b_i": nrm((L_, MLSTM_HEADS), 0.1),
        "mlstm_b_f": jnp.linspace(3.0, 6.0, MLSTM_HEADS, dtype=jnp.float32) + nrm((L_, MLSTM_HEADS), 0.1),
        "w_out": nrm((L_, D_MIX, D_MODEL), D_MIX ** -0.5),
        "norm_post_mix": gain((L_, D_MODEL)),
        "norm_pre_ffn": gain((L_, D_MODEL)),
        "w_gate": nrm((L_, D_MODEL, D_FF), D_MODEL ** -0.5),
        "w_up": nrm((L_, D_MODEL, D_FF), D_MODEL ** -0.5),
        "w_down": nrm((L_, D_FF, D_MODEL), D_FF ** -0.5),
        "norm_post_ffn": gain((L_, D_MODEL)),
    }


def reference(x, norm_pre_mix, w_in, s5_lambda_re, s5_lambda_im, s5_log_step, s5_b_re, s5_b_im,
              s5_c_re, s5_c_im, s5_d, s5_w_glu, s5_b_glu, ssd_conv_w, ssd_conv_b, ssd_dt_bias,
              ssd_a_log, ssd_d, ssd_norm, mlstm_conv_w, mlstm_conv_b, mlstm_b_i, mlstm_b_f,
              w_out, norm_post_mix, norm_pre_ffn, w_gate, w_up, w_down, norm_post_ffn):
    for l in range(DEPTH):
        x = hybrid_layer(x, norm_pre_mix[l], w_in[l], s5_lambda_re[l], s5_lambda_im[l], s5_log_step[l],
                         s5_b_re[l], s5_b_im[l], s5_c_re[l], s5_c_im[l], s5_d[l], s5_w_glu[l],
                         s5_b_glu[l], ssd_conv_w[l], ssd_conv_b[l], ssd_dt_bias[l], ssd_a_log[l],
                         ssd_d[l], ssd_norm[l], mlstm_conv_w[l], mlstm_conv_b[l], mlstm_b_i[l],
                         mlstm_b_f[l], w_out[l], norm_post_mix[l], norm_pre_ffn[l], w_gate[l],
                         w_up[l], w_down[l], norm_post_ffn[l])
    return x
```

```python
import functools
import math

import jax
import jax.numpy as jnp
from jax import lax
from jax.experimental import pallas as pl
from jax.experimental.pallas import tpu as pltpu

F32 = jnp.float32
BF16 = jnp.bfloat16

D_MODEL = 1024
D_MIX = 2048
S5_WIDTH = 512
S5_GROUP = 16
S5_GROUPS = 32
S5_STATE = 64
S5_NSTATE = S5_GROUPS * S5_STATE
SSD_HEADDIM = 64
SSD_WIDTH = 768
SSD_HEADS = 12
SSD_GROUPS = 2
SSD_HPG = SSD_HEADS // SSD_GROUPS
SSD_GW = SSD_WIDTH // SSD_GROUPS
SSD_STATE = 128
SSD_XBC = SSD_WIDTH + 2 * SSD_GROUPS * SSD_STATE
MLSTM_WIDTH = 768
MLSTM_HEADS = 6
MLSTM_VDIM = 128
MLSTM_QKDIM = 64
MLSTM_QK = 384
CONV_K = 4
D_FF = 2816
NORM_EPS = 1e-6
IN_SPLIT_SIZES = (S5_WIDTH, SSD_WIDTH, SSD_XBC, SSD_HEADS, 2 * MLSTM_QK, MLSTM_WIDTH,
                  MLSTM_HEADS, MLSTM_HEADS, MLSTM_WIDTH)

LANES = 128
CHUNK = 128
HALO = 8
GATE_DT = 0
GATE_I = SSD_HEADS
GATE_F = SSD_HEADS + MLSTM_HEADS
P_U = 0
P_ZX = P_U + S5_WIDTH
P_QK = P_ZX + SSD_WIDTH + SSD_XBC
P_V = P_QK + 2 * MLSTM_QK
P_O = P_V + MLSTM_WIDTH
P_G = P_O + MLSTM_WIDTH
D_INP = P_G + LANES

VMEM_LIMIT = 56 * 1024 * 1024


def _silu(x):
    return x * jax.nn.sigmoid(x)


def _softplus(x):
    return jnp.maximum(x, 0.0) + jnp.log1p(jnp.exp(-jnp.abs(x)))


def _rms(x, g):
    return x * lax.rsqrt(jnp.mean(x * x, axis=-1, keepdims=True) + NORM_EPS) * g


def _dot(a, b):
    return jnp.dot(a, b, preferred_element_type=F32)


def _dot_nt(a, b):
    return lax.dot_general(a, b, (((1,), (1,)), ((), ())), preferred_element_type=F32)


def _dot_tn(a, b):
    return lax.dot_general(a, b, (((0,), (0,)), ((), ())), preferred_element_type=F32)


def _cumsum_rows(x):
    t = x.shape[0]
    r = lax.broadcasted_iota(jnp.int32, (t, t), 0)
    c = lax.broadcasted_iota(jnp.int32, (t, t), 1)
    tril = jnp.where(c <= r, 1.0, 0.0).astype(F32)
    return jnp.dot(tril, x, precision=lax.Precision.HIGHEST, preferred_element_type=F32)


def _expand_heads(v, first, heads, width):
    t = v.shape[0]
    per = LANES // width
    lane = lax.broadcasted_iota(jnp.int32, (t, LANES), 1)
    outs = []
    for p in range(heads // per):
        acc = None
        for q in range(per):
            h = first + p * per + q
            col = jnp.broadcast_to(v[:, h:h + 1], (t, LANES))
            acc = col if acc is None else jnp.where(lane < q * width, acc, col)
        outs.append(acc)
    return jnp.concatenate(outs, axis=1) if len(outs) > 1 else outs[0]


def _inproj_kernel(x_ref, g_ref, w_ref, u_ref, zx_ref, qk_ref, v_ref, o_ref, gt_ref):
    h = _rms(x_ref[0], g_ref[...]).astype(BF16)
    u_ref[...] = _dot(h, w_ref[:, P_U:P_ZX])
    zx_ref[0] = _dot(h, w_ref[:, P_ZX:P_QK])
    qk_ref[0] = _dot(h, w_ref[:, P_QK:P_V])
    v_ref[0] = _dot(h, w_ref[:, P_V:P_O]).astype(BF16)
    o_ref[0] = _dot(h, w_ref[:, P_O:P_G])
    gt_ref[0] = _dot(h, w_ref[:, P_G:D_INP])


def _in_proj(x, g, w, *, tt, interpret=False):
    b, l, d = x.shape
    grid = (b, l // tt)
    tok = lambda width: pl.BlockSpec((1, tt, width), lambda i, j: (i, j, 0))
    return pl.pallas_call(
        _inproj_kernel,
        grid=grid,
        in_specs=[tok(d),
                  pl.BlockSpec((1, d), lambda i, j: (0, 0)),
                  pl.BlockSpec((d, D_INP), lambda i, j: (0, 0))],
        out_specs=[pl.BlockSpec((tt, S5_WIDTH), lambda i, j: (j, i)),
                   tok(SSD_WIDTH + SSD_XBC), tok(2 * MLSTM_QK), tok(MLSTM_WIDTH),
                   tok(MLSTM_WIDTH), tok(LANES)],
        out_shape=[jax.ShapeDtypeStruct((l, b * S5_WIDTH), F32),
                   jax.ShapeDtypeStruct((b, l, SSD_WIDTH + SSD_XBC), F32),
                   jax.ShapeDtypeStruct((b, l, 2 * MLSTM_QK), F32),
                   jax.ShapeDtypeStruct((b, l, MLSTM_WIDTH), BF16),
                   jax.ShapeDtypeStruct((b, l, MLSTM_WIDTH), F32),
                   jax.ShapeDtypeStruct((b, l, LANES), F32)],
        compiler_params=pltpu.CompilerParams(
            dimension_semantics=("parallel", "parallel"), vmem_limit_bytes=VMEM_LIMIT),
        name="in_proj",
        interpret=interpret,
    )(x, g, w)


S5_KB = 2
S5_KW = S5_WIDTH // S5_KB
S5_HALF = S5_NSTATE // S5_KB
S5_STRIP = 256


def _s5_kernel(u_ref, bblk_ref, cblk_ref, are_ref, aim_ref, d_ref, wglu_ref, bglu_ref,
               y_ref, bu_ref, xs_ref, st_ref, *, nb, tc):
    @pl.when(pl.program_id(0) == 0)
    def _():
        st_ref[...] = jnp.zeros_like(st_ref)

    u = u_ref[...]
    ub = u.astype(BF16)
    for kb in range(S5_KB):
        bu_ref[:, kb * 2 * S5_HALF:(kb + 1) * 2 * S5_HALF] = _dot(
            ub[:, kb * S5_KW:(kb + 1) * S5_KW], bblk_ref[kb])

    for kb in range(S5_KB):
        for s in range(S5_HALF // S5_STRIP):
            c_re = kb * 2 * S5_HALF + s * S5_STRIP
            c_im = c_re + S5_HALF
            c_a = kb * S5_HALF + s * S5_STRIP
            ar = jnp.broadcast_to(are_ref[:, c_a:c_a + S5_STRIP], (nb, S5_STRIP))
            ai = jnp.broadcast_to(aim_ref[:, c_a:c_a + S5_STRIP], (nb, S5_STRIP))

            def step(t, carry, c_re=c_re, c_im=c_im, ar=ar, ai=ai):
                xr, xi = carry
                r0 = pl.multiple_of(t * nb, nb)
                br = bu_ref[pl.ds(r0, nb), c_re:c_re + S5_STRIP]
                bi = bu_ref[pl.ds(r0, nb), c_im:c_im + S5_STRIP]
                nr = ar * xr - ai * xi + br
                ni = ar * xi + ai * xr + bi
                xs_ref[pl.ds(r0, nb), c_re:c_re + S5_STRIP] = nr.astype(BF16)
                xs_ref[pl.ds(r0, nb), c_im:c_im + S5_STRIP] = ni.astype(BF16)
                return nr, ni

            xr, xi = lax.fori_loop(
                0, tc, step,
                (st_ref[:, c_re:c_re + S5_STRIP], st_ref[:, c_im:c_im + S5_STRIP]), unroll=4)
            st_ref[:, c_re:c_re + S5_STRIP] = xr
            st_ref[:, c_im:c_im + S5_STRIP] = xi

    ys = [_dot(xs_ref[:, kb * 2 * S5_HALF:(kb + 1) * 2 * S5_HALF], cblk_ref[kb])
          for kb in range(S5_KB)]
    y = jnp.concatenate(ys, axis=1) + d_ref[...] * u
    g = jax.nn.gelu(y)
    gate = jax.nn.sigmoid(_dot(g.astype(BF16), wglu_ref[...]) + bglu_ref[...])
    y_ref[...] = (g * gate).astype(BF16)


def _s5(u_tm, bblk, cblk, a_re, a_im, d_row, w_glu, b_glu, *, nb, tc, interpret=False):
    rows = u_tm.shape[0]
    m = nb * tc
    full = lambda a: pl.BlockSpec(a.shape, lambda i, _n=a.ndim: (0,) * _n)
    return pl.pallas_call(
        functools.partial(_s5_kernel, nb=nb, tc=tc),
        grid=(rows // m,),
        in_specs=[pl.BlockSpec((m, S5_WIDTH), lambda i: (i, 0)),
                  full(bblk), full(cblk), full(a_re), full(a_im), full(d_row),
                  full(w_glu), full(b_glu)],
        out_specs=pl.BlockSpec((m, S5_WIDTH), lambda i: (i, 0)),
        out_shape=jax.ShapeDtypeStruct((rows, S5_WIDTH), BF16),
        scratch_shapes=[pltpu.VMEM((m, 2 * S5_NSTATE), F32),
                        pltpu.VMEM((m, 2 * S5_NSTATE), BF16),
                        pltpu.VMEM((nb, 2 * S5_NSTATE), F32)],
        compiler_params=pltpu.CompilerParams(
            dimension_semantics=("arbitrary",), vmem_limit_bytes=VMEM_LIMIT),
        name="s5",
        interpret=interpret,
    )(u_tm, bblk, cblk, a_re, a_im, d_row, w_glu, b_glu)


def _s5_prepare(lam_re, lam_im, log_step, b_re, b_im, c_re, c_im):
    step = jnp.exp(log_step)[:, None]
    mag = jnp.exp(lam_re * step)
    ang = lam_im * step
    a_re = mag * jnp.cos(ang)
    a_im = mag * jnp.sin(ang)
    den = lam_re * lam_re + lam_im * lam_im
    nr = a_re - 1.0
    f_re = (nr * lam_re + a_im * lam_im) / den
    f_im = (a_im * lam_re - nr * lam_im) / den
    bb_re = f_re[..., None] * b_re - f_im[..., None] * b_im
    bb_im = f_re[..., None] * b_im + f_im[..., None] * b_re
    gl = S5_GROUPS // S5_KB
    eye = jnp.eye(gl, dtype=F32)

    def b_block(bb):
        bb = bb.reshape(S5_KB, gl, S5_STATE, S5_GROUP)
        blk = jnp.einsum('kgph,gq->kghqp', bb, eye)
        return blk.reshape(S5_KB, gl * S5_GROUP, gl * S5_STATE)

    def c_block(cc):
        cc = cc.reshape(S5_KB, gl, S5_GROUP, S5_STATE)
        blk = jnp.einsum('kghp,gq->kgpqh', cc, eye)
        return blk.reshape(S5_KB, gl * S5_STATE, gl * S5_GROUP)

    bblk = jnp.concatenate([b_block(bb_re), b_block(bb_im)], axis=2).astype(BF16)
    cblk = jnp.concatenate([c_block(c_re), -c_block(c_im)], axis=1).astype(BF16)
    return bblk, cblk, a_re.reshape(1, S5_NSTATE), a_im.reshape(1, S5_NSTATE)


def _causal_conv_chunk(cbuf, x, w_ref, b_ref, first):
    t = x.shape[0]

    @pl.when(first)
    def _():
        cbuf[0:HALO, :] = jnp.zeros((HALO, cbuf.shape[1]), F32)

    cbuf[HALO:HALO + t, :] = x
    acc = b_ref[...]
    for j in range(CONV_K):
        off = HALO - (CONV_K - 1) + j
        acc = acc + w_ref[j:j + 1, :] * cbuf[off:off + t, :]
    cbuf[0:HALO, :] = cbuf[t:t + HALO, :]
    return acc


def _ssd_kernel(zx_ref, gt_ref, cw_ref, cb_ref, dtb_ref, alog_ref, dsk_ref, ng_ref,
                y_ref, cbuf, hst):
    t = CHUNK
    first = pl.program_id(1) == 0

    @pl.when(first)
    def _():
        hst[...] = jnp.zeros_like(hst)

    z = zx_ref[0, :, 0:SSD_WIDTH]
    xbc = _silu(_causal_conv_chunk(cbuf, zx_ref[0, :, SSD_WIDTH:], cw_ref, cb_ref, first))
    xs = xbc[:, 0:SSD_WIDTH]
    dt = _softplus(gt_ref[0] + dtb_ref[...])
    acs = _cumsum_rows(dt * (-jnp.exp(alog_ref[...])))
    acs_t = acs.T
    dt_e = _expand_heads(dt, GATE_DT, SSD_HEADS, SSD_HEADDIM)
    acs_e = _expand_heads(acs, GATE_DT, SSD_HEADS, SSD_HEADDIM)
    last_e = acs_e[t - 1:t, :]
    xdt = xs * dt_e
    xw_b = (xdt * jnp.exp(last_e - acs_e)).astype(BF16)
    eacs_e = jnp.exp(acs_e)
    cdec_e = jnp.exp(last_e)

    row = lax.broadcasted_iota(jnp.int32, (t, t), 0)
    col = lax.broadcasted_iota(jnp.int32, (t, t), 1)
    causal = col <= row
    lane = lax.broadcasted_iota(jnp.int32, (t, LANES), 1)
    ys = []
    for g in range(SSD_GROUPS):
        b0 = SSD_WIDTH + g * SSD_STATE
        c0 = SSD_WIDTH + SSD_GROUPS * SSD_STATE + g * SSD_STATE
        bm = xbc[:, b0:b0 + SSD_STATE].astype(BF16)
        cm = xbc[:, c0:c0 + SSD_STATE].astype(BF16)
        cbm = _dot_nt(cm, bm)
        gs = slice(g * SSD_GW, (g + 1) * SSD_GW)
        h_prev = hst[g]
        y_g = _dot(cm, h_prev.astype(BF16)) * eacs_e[:, gs]
        diag = []
        for pr in range(SSD_HPG // 2):
            ps = slice(g * SSD_GW + pr * LANES, g * SSD_GW + (pr + 1) * LANES)
            xp = xdt[:, ps]
            acc = None
            for q in range(2):
                j = g * SSD_HPG + pr * 2 + q
                seg = acs[:, j:j + 1] - acs_t[j:j + 1, :]
                m = (cbm * jnp.exp(jnp.where(causal, seg, -jnp.inf))).astype(BF16)
                keep = (lane < SSD_HEADDIM) if q == 0 else (lane >= SSD_HEADDIM)
                part = _dot(m, jnp.where(keep, xp, 0.0).astype(BF16))
                acc = part if acc is None else acc + part
            diag.append(acc)
        ys.append(y_g + jnp.concatenate(diag, axis=1))
        hst[g] = h_prev * cdec_e[:, gs] + _dot_tn(bm, xw_b[:, gs])
    y = jnp.concatenate(ys, axis=1) + xs * dsk_ref[...]
    y = y * _silu(z)
    outs = []
    for g in range(SSD_GROUPS):
        gs = slice(g * SSD_GW, (g + 1) * SSD_GW)
        outs.append(_rms(y[:, gs], ng_ref[:, gs]))
    y_ref[0] = jnp.concatenate(outs, axis=1).astype(BF16)


def _ssd(zx, gates, conv_w, conv_b, dtb, alog, dsk, ng, *, interpret=False):
    b, l, _ = zx.shape
    full = lambda a: pl.BlockSpec(a.shape, lambda i, j, _n=a.ndim: (0,) * _n)
    tok = lambda width: pl.BlockSpec((1, CHUNK, width), lambda i, j: (i, j, 0))
    return pl.pallas_call(
        _ssd_kernel,
        grid=(b, l // CHUNK),
        in_specs=[tok(SSD_WIDTH + SSD_XBC), tok(LANES), full(conv_w), full(conv_b),
                  full(dtb), full(alog), full(dsk), full(ng)],
        out_specs=tok(SSD_WIDTH),
        out_shape=jax.ShapeDtypeStruct((b, l, SSD_WIDTH), BF16),
        scratch_shapes=[pltpu.VMEM((CHUNK + HALO, SSD_XBC), F32),
                        pltpu.VMEM((SSD_GROUPS, SSD_STATE, SSD_GW), F32)],
        compiler_params=pltpu.CompilerParams(
            dimension_semantics=("parallel", "arbitrary"), vmem_limit_bytes=VMEM_LIMIT),
        name="ssd",
        interpret=interpret,
    )(zx, gates, conv_w, conv_b, dtb, alog, dsk, ng)


def _mlstm_kernel(qk_ref, v_ref, o_ref, gt_ref, cw_ref, cb_ref, bi_ref, bf_ref,
                  y_ref, cbuf, cst, nst, mst):
    t = CHUNK
    first = pl.program_id(1) == 0

    @pl.when(first)
    def _():
        cst[...] = jnp.zeros_like(cst)
        nst[...] = jnp.zeros_like(nst)
        mst[...] = jnp.zeros_like(mst)

    qk = _silu(_causal_conv_chunk(cbuf, qk_ref[0], cw_ref, cb_ref, first))
    q = qk[:, 0:MLSTM_QK]
    k = qk[:, MLSTM_QK:] * (MLSTM_QKDIM ** -0.5)
    gates = gt_ref[0]
    log_i = gates + bi_ref[...]
    z = gates + bf_ref[...]
    log_f = -_softplus(-z)
    log_f = pltpu.roll(log_f, LANES - (GATE_F - GATE_I), axis=1)
    bcum = _cumsum_rows(log_f)
    g_tot = bcum[t - 1:t, :]
    m_prev = mst[...]
    w_end = g_tot - bcum + log_i
    a_loc = jnp.max(w_end, axis=0, keepdims=True)
    e_end = jnp.exp(w_end - a_loc)
    w_inter = bcum + m_prev
    m_new = jnp.maximum(g_tot + m_prev, a_loc)
    s_old = jnp.exp(g_tot + m_prev - m_new)
    s_new = jnp.exp(a_loc - m_new)
    bcum_t = bcum.T
    log_i_t = log_i.T

    e_end_e = _expand_heads(e_end, GATE_I, MLSTM_HEADS, MLSTM_QKDIM)
    kw = k * e_end_e
    n_prev = nst[...]
    qn = q * n_prev
    n_loc = jnp.sum(kw, axis=0, keepdims=True)
    s_old_e = _expand_heads(s_old, GATE_I, MLSTM_HEADS, MLSTM_QKDIM)
    s_new_e = _expand_heads(s_new, GATE_I, MLSTM_HEADS, MLSTM_QKDIM)
    nst[...] = s_old_e * n_prev + s_new_e * n_loc
    mst[...] = m_new

    row = lax.broadcasted_iota(jnp.int32, (t, t), 0)
    col = lax.broadcasted_iota(jnp.int32, (t, t), 1)
    causal = col <= row
    lane = lax.broadcasted_iota(jnp.int32, (t, LANES), 1)
    srow = lax.broadcasted_iota(jnp.int32, (LANES, MLSTM_VDIM), 0)
    kb = k.astype(BF16)
    for pr in range(MLSTM_HEADS // 2):
        ps = slice(pr * LANES, (pr + 1) * LANES)
        c_prev = cst[pr]
        c_prev_b = c_prev.astype(BF16)
        c_upd = None
        for hq in range(2):
            h = pr * 2 + hq
            ln = GATE_I + h
            keep = (lane < MLSTM_QKDIM) if hq == 0 else (lane >= MLSTM_QKDIM)
            q_h = jnp.where(keep, q[:, ps], 0.0)
            q_hb = q_h.astype(BF16)
            v_h = v_ref[0, :, h * MLSTM_VDIM:(h + 1) * MLSTM_VDIM]
            s = _dot_nt(q_hb, kb[:, ps])
            dmat = bcum[:, ln:ln + 1] - bcum_t[ln:ln + 1, :] + log_i_t[ln:ln + 1, :]
            dmat = jnp.where(causal, dmat, -jnp.inf)
            w_int = w_inter[:, ln:ln + 1]
            m_row = jnp.maximum(w_int, jnp.max(dmat, axis=1, keepdims=True))
            s_int = jnp.exp(w_int - m_row)
            qs = s * jnp.exp(dmat - m_row)
            num = _dot(qs.astype(BF16), v_h) + s_int * _dot(q_hb, c_prev_b)
            den = (jnp.sum(qs, axis=1, keepdims=True)
                   + s_int * jnp.sum(jnp.where(keep, qn[:, ps], 0.0), axis=1, keepdims=True))
            den = jnp.maximum(jnp.abs(den), jnp.exp(-m_row))
            o_h = o_ref[0, :, h * MLSTM_VDIM:(h + 1) * MLSTM_VDIM]
            y_ref[0, :, h * MLSTM_VDIM:(h + 1) * MLSTM_VDIM] = (
                jax.nn.sigmoid(o_h) * (num / den)).astype(BF16)
            part = _dot_tn(jnp.where(keep, kw[:, ps], 0.0).astype(BF16), v_h)
            c_upd = part if c_upd is None else c_upd + part
        so = jnp.where(srow < MLSTM_QKDIM, s_old[:, GATE_I + 2 * pr:GATE_I + 2 * pr + 1],
                       s_old[:, GATE_I + 2 * pr + 1:GATE_I + 2 * pr + 2])
        sn = jnp.where(srow < MLSTM_QKDIM, s_new[:, GATE_I + 2 * pr:GATE_I + 2 * pr + 1],
                       s_new[:, GATE_I + 2 * pr + 1:GATE_I + 2 * pr + 2])
        cst[pr] = so * c_prev + sn * c_upd


def _mlstm(qk, v, o, gates, conv_w, conv_b, bi, bf, *, interpret=False):
    b, l, _ = qk.shape
    full = lambda a: pl.BlockSpec(a.shape, lambda i, j, _n=a.ndim: (0,) * _n)
    tok = lambda width: pl.BlockSpec((1, CHUNK, width), lambda i, j: (i, j, 0))
    return pl.pallas_call(
        _mlstm_kernel,
        grid=(b, l // CHUNK),
        in_specs=[tok(2 * MLSTM_QK), tok(MLSTM_WIDTH), tok(MLSTM_WIDTH), tok(LANES),
                  full(conv_w), full(conv_b), full(bi), full(bf)],
        out_specs=tok(MLSTM_WIDTH),
        out_shape=jax.ShapeDtypeStruct((b, l, MLSTM_WIDTH), BF16),
        scratch_shapes=[pltpu.VMEM((CHUNK + HALO, 2 * MLSTM_QK), F32),
                        pltpu.VMEM((MLSTM_HEADS // 2, LANES, MLSTM_VDIM), F32),
                        pltpu.VMEM((1, MLSTM_QK), F32),
                        pltpu.VMEM((1, LANES), F32)],
        compiler_params=pltpu.CompilerParams(
            dimension_semantics=("parallel", "arbitrary"), vmem_limit_bytes=VMEM_LIMIT),
        name="mlstm",
        interpret=interpret,
    )(qk, v, o, gates, conv_w, conv_b, bi, bf)


def _outffn_kernel(x_ref, ya_ref, yb_ref, yc_ref, wo_ref, g1_ref, g2_ref, wg_ref, wu_ref,
                   wd_ref, g3_ref, o_ref):
    a0 = S5_WIDTH
    a1 = S5_WIDTH + SSD_WIDTH
    mix = (_dot(ya_ref[...], wo_ref[0:a0, :]) + _dot(yb_ref[0], wo_ref[a0:a1, :])
           + _dot(yc_ref[0], wo_ref[a1:, :]))
    x1 = x_ref[0] + _rms(mix, g1_ref[...])
    h = _rms(x1, g2_ref[...]).astype(BF16)
    act = (_silu(_dot(h, wg_ref[...])) * _dot(h, wu_ref[...])).astype(BF16)
    ff = _dot(act, wd_ref[...])
    o_ref[0] = x1 + _rms(ff, g3_ref[...])


def _out_ffn(x, ya_tm, yb, yc, wo, g1, g2, wg, wu, wd, g3, *, tt, interpret=False):
    b, l, d = x.shape
    tok = lambda width: pl.BlockSpec((1, tt, width), lambda i, j: (i, j, 0))
    const = lambda a: pl.BlockSpec(a.shape, lambda i, j, _n=a.ndim: (0,) * _n,
                                   pipeline_mode=pl.Buffered(1))
    return pl.pallas_call(
        _outffn_kernel,
        grid=(b, l // tt),
        in_specs=[tok(d), pl.BlockSpec((tt, S5_WIDTH), lambda i, j: (j, i)),
                  tok(SSD_WIDTH), tok(MLSTM_WIDTH), const(wo), const(g1), const(g2),
                  const(wg), const(wu), const(wd), const(g3)],
        out_specs=tok(d),
        out_shape=jax.ShapeDtypeStruct((b, l, d), F32),
        compiler_params=pltpu.CompilerParams(
            dimension_semantics=("parallel", "parallel"), vmem_limit_bytes=VMEM_LIMIT),
        name="out_ffn",
        interpret=interpret,
    )(x, ya_tm, yb, yc, wo, g1, g2, wg, wu, wd, g3)


def _lane_row(vals, offset):
    return jnp.zeros((1, LANES), F32).at[0, offset:offset + vals.shape[0]].set(vals)


def _pack_w_in(w_in):
    offs = [0]
    for s in IN_SPLIT_SIZES:
        offs.append(offs[-1] + s)
    seg = lambda i: w_in[..., offs[i]:offs[i + 1]]
    u, z, xbc, dt, qk, v, gi, gf, o = (seg(i) for i in range(9))
    pad = jnp.zeros(w_in.shape[:-1] + (LANES - SSD_HEADS - 2 * MLSTM_HEADS,), w_in.dtype)
    return jnp.concatenate([u, z, xbc, qk, v, o, dt, gi, gf, pad], axis=-1).astype(BF16)


def _layer(x, p, *, tt, tc, interpret):
    b, l, _ = x.shape
    u_tm, zx, qk, v, o, gates = _in_proj(x, p["norm_pre_mix"], p["w_in"], tt=tt,
                                         interpret=interpret)
    ya = _s5(u_tm.reshape(l * b, S5_WIDTH), p["s5_bblk"], p["s5_cblk"], p["s5_are"], p["s5_aim"],
             p["s5_d"], p["s5_w_glu"], p["s5_b_glu"], nb=b, tc=tc, interpret=interpret)
    yb = _ssd(zx, gates, p["ssd_conv_w"], p["ssd_conv_b"], p["ssd_dtb"], p["ssd_alog"],
              p["ssd_dsk"], p["ssd_norm"], interpret=interpret)
    yc = _mlstm(qk, v, o, gates, p["mlstm_conv_w"], p["mlstm_conv_b"], p["mlstm_bi"],
                p["mlstm_bf"], interpret=interpret)
    return _out_ffn(x, ya.reshape(l, b * S5_WIDTH), yb, yc, p["w_out"], p["norm_post_mix"],
                    p["norm_pre_ffn"], p["w_gate"], p["w_up"], p["w_down"], p["norm_post_ffn"],
                    tt=tt, interpret=interpret)


def _prepare_params(norm_pre_mix, w_in, s5_lambda_re, s5_lambda_im, s5_log_step, s5_b_re, s5_b_im,
                    s5_c_re, s5_c_im, s5_d, s5_w_glu, s5_b_glu, ssd_conv_w, ssd_conv_b,
                    ssd_dt_bias, ssd_a_log, ssd_d, ssd_norm, mlstm_conv_w, mlstm_conv_b, mlstm_b_i,
                    mlstm_b_f, w_out, norm_post_mix, norm_pre_ffn, w_gate, w_up, w_down,
                    norm_post_ffn):
    depth = w_in.shape[0]
    row = lambda a: a.reshape(depth, 1, -1).astype(F32)
    bblk, cblk, a_re, a_im = jax.vmap(_s5_prepare)(
        s5_lambda_re.astype(F32), s5_lambda_im.astype(F32), s5_log_step.astype(F32),
        s5_b_re.astype(F32), s5_b_im.astype(F32), s5_c_re.astype(F32), s5_c_im.astype(F32))
    lanes = lambda a, off: jax.vmap(lambda v: _lane_row(v.astype(F32), off))(a)
    return dict(
        norm_pre_mix=row(norm_pre_mix), w_in=_pack_w_in(w_in),
        s5_bblk=bblk, s5_cblk=cblk, s5_are=a_re, s5_aim=a_im, s5_d=row(s5_d),
        s5_w_glu=s5_w_glu.astype(BF16), s5_b_glu=row(s5_b_glu),
        ssd_conv_w=ssd_conv_w.astype(F32), ssd_conv_b=row(ssd_conv_b),
        ssd_dtb=lanes(ssd_dt_bias, GATE_DT), ssd_alog=lanes(ssd_a_log, GATE_DT),
        ssd_dsk=row(jnp.repeat(ssd_d, SSD_HEADDIM, axis=-1)), ssd_norm=row(ssd_norm),
        mlstm_conv_w=mlstm_conv_w.astype(F32), mlstm_conv_b=row(mlstm_conv_b),
        mlstm_bi=lanes(mlstm_b_i, GATE_I), mlstm_bf=lanes(mlstm_b_f, GATE_F),
        w_out=w_out.astype(BF16), norm_post_mix=row(norm_post_mix),
        norm_pre_ffn=row(norm_pre_ffn), w_gate=w_gate.astype(BF16), w_up=w_up.astype(BF16),
        w_down=w_down.astype(BF16), norm_post_ffn=row(norm_post_ffn))


def _forward(x, *params, tt=256, tc=32, interpret=False):
    stacked = _prepare_params(*params)

    def body(xc, p):
        return _layer(xc, p, tt=tt, tc=tc, interpret=interpret), None

    out, _ = lax.scan(body, x, stacked)
    return out


def kernel(x, norm_pre_mix, w_in, s5_lambda_re, s5_lambda_im, s5_log_step, s5_b_re, s5_b_im, s5_c_re, s5_c_im, s5_d, s5_w_glu, s5_b_glu, ssd_conv_w, ssd_conv_b, ssd_dt_bias, ssd_a_log, ssd_d, ssd_norm, mlstm_conv_w, mlstm_conv_b, mlstm_b_i, mlstm_b_f, w_out, norm_post_mix, norm_pre_ffn, w_gate, w_up, w_down, norm_post_ffn):
    return _forward(x, norm_pre_mix, w_in, s5_lambda_re, s5_lambda_im, s5_log_step, s5_b_re,
                    s5_b_im, s5_c_re, s5_c_im, s5_d, s5_w_glu, s5_b_glu, ssd_conv_w, ssd_conv_b,
                    ssd_dt_bias, ssd_a_log, ssd_d, ssd_norm, mlstm_conv_w, mlstm_conv_b, mlstm_b_i,
                    mlstm_b_f, w_out, norm_post_mix, norm_pre_ffn, w_gate, w_up, w_down,
                    norm_post_ffn)
```

```python
import functools
import math

import jax
import jax.numpy as jnp
from jax import lax
from jax.experimental import pallas as pl
from jax.experimental.pallas import tpu as pltpu

F32 = jnp.float32
BF16 = jnp.bfloat16

D_MODEL = 1024
D_MIX = 2048
S5_WIDTH = 512
S5_GROUP = 16
S5_GROUPS = 32
S5_STATE = 64
S5_NSTATE = S5_GROUPS * S5_STATE
SSD_HEADDIM = 64
SSD_WIDTH = 768
SSD_HEADS = 12
SSD_GROUPS = 2
SSD_HPG = SSD_HEADS // SSD_GROUPS
SSD_GW = SSD_WIDTH // SSD_GROUPS
SSD_STATE = 128
SSD_XBC = SSD_WIDTH + 2 * SSD_GROUPS * SSD_STATE
MLSTM_WIDTH = 768
MLSTM_HEADS = 6
MLSTM_VDIM = 128
MLSTM_QKDIM = 64
MLSTM_QK = 384
CONV_K = 4
D_FF = 2816
NORM_EPS = 1e-6
IN_SPLIT_SIZES = (S5_WIDTH, SSD_WIDTH, SSD_XBC, SSD_HEADS, 2 * MLSTM_QK, MLSTM_WIDTH,
                  MLSTM_HEADS, MLSTM_HEADS, MLSTM_WIDTH)

LANES = 128
CHUNK = 128
HALO = 8
LOG2E = math.log2(math.e)
GATE_DT = 0
GATE_I = SSD_HEADS
GATE_F = SSD_HEADS + MLSTM_HEADS
P_U = 0
P_Z = P_U + S5_WIDTH
P_CONV = P_Z + SSD_WIDTH
P_V = P_CONV + SSD_XBC + 2 * MLSTM_QK
P_O = P_V + MLSTM_WIDTH
P_G = P_O + MLSTM_WIDTH
D_INP = P_G + LANES
CONV_W = SSD_XBC + 2 * MLSTM_QK

VMEM_LIMIT = 56 * 1024 * 1024


def _silu(x):
    hx = 0.5 * x
    return hx + hx * jnp.tanh(hx)


def _sigmoid(x):
    return 0.5 + 0.5 * jnp.tanh(0.5 * x)


def _softplus(x):
    return jnp.maximum(x, 0.0) + jnp.log1p(jnp.exp(-jnp.abs(x)))


def _rms(x, g):
    return x * lax.rsqrt(jnp.mean(x * x, axis=-1, keepdims=True) + NORM_EPS) * g


def _dot(a, b):
    return jnp.dot(a, b, preferred_element_type=F32)


def _dot_nt(a, b):
    return lax.dot_general(a, b, (((1,), (1,)), ((), ())), preferred_element_type=F32)


def _dot_tn(a, b):
    return lax.dot_general(a, b, (((0,), (0,)), ((), ())), preferred_element_type=F32)


def _cumsum_rows(x):
    t = x.shape[0]
    r = lax.broadcasted_iota(jnp.int32, (t, t), 0)
    c = lax.broadcasted_iota(jnp.int32, (t, t), 1)
    tril = jnp.where(c <= r, 1.0, 0.0).astype(F32)
    return jnp.dot(tril, x, precision=lax.Precision.HIGHEST, preferred_element_type=F32)


def _cummax_rows(x):
    t = x.shape[0]
    row = lax.broadcasted_iota(jnp.int32, x.shape, 0)
    d = 1
    while d < t:
        x = jnp.maximum(x, jnp.where(row >= d, pltpu.roll(x, d, axis=0), -jnp.inf))
        d *= 2
    return x


def _expand_heads(v, first, heads, width):
    t = v.shape[0]
    per = LANES // width
    lane = lax.broadcasted_iota(jnp.int32, (t, LANES), 1)
    outs = []
    for p in range(heads // per):
        acc = None
        for q in range(per):
            h = first + p * per + q
            col = jnp.broadcast_to(v[:, h:h + 1], (t, LANES))
            acc = col if acc is None else jnp.where(lane < q * width, acc, col)
        outs.append(acc)
    return jnp.concatenate(outs, axis=1) if len(outs) > 1 else outs[0]


def _half_masks(t, width):
    lane = lax.broadcasted_iota(jnp.int32, (t, LANES), 1)
    lo = jnp.where(lane < width, 1.0, 0.0).astype(BF16)
    hi = jnp.where(lane >= width, 1.0, 0.0).astype(BF16)
    return lo, hi


CONV_CW = 256


def _shift_rows(x3, prev3, j):
    rolled = pltpu.roll(jnp.concatenate([prev3, x3], axis=0), j, axis=1)
    sub = lax.broadcasted_iota(jnp.int32, x3.shape, 1)
    return jnp.where(sub >= j, rolled[1:], rolled[:-1])


def _causal_conv4(x3, w_ref, cs):
    zero = jnp.zeros((1,) + x3.shape[1:], F32)
    x1 = _shift_rows(x3, zero, 1)
    a = w_ref[3:4, cs] * x3[1:] + w_ref[2:3, cs] * x1[1:]
    b = w_ref[1:2, cs] * x3 + w_ref[0:1, cs] * x1
    return a + _shift_rows(b[1:], b[0:1], 2)


def _inproj_kernel(x_ref, g_ref, w_ref, cw_ref, cb_ref, cs_ref,
                   u_ref, zx_ref, qk_ref, v_ref, so_ref, gt_ref, carry):
    t = x_ref.shape[1]

    @pl.when(pl.program_id(1) == 0)
    def _():
        carry[...] = jnp.zeros_like(carry)

    h = _rms(x_ref[0], g_ref[...]).astype(BF16)
    u_ref[0] = _dot(h, w_ref[:, P_U:P_Z])
    zx_ref[0, :, 0:SSD_WIDTH] = _silu(_dot(h, w_ref[:, P_Z:P_CONV]))
    for c in range(CONV_W // CONV_CW):
        cs = slice(c * CONV_CW, (c + 1) * CONV_CW)
        pc = _dot(h, w_ref[:, P_CONV + c * CONV_CW:P_CONV + (c + 1) * CONV_CW])
        x3 = jnp.concatenate([carry[:, cs], pc], axis=0).reshape(t // HALO + 1, HALO, CONV_CW)
        carry[:, cs] = pc[t - HALO:, :]
        acc = _causal_conv4(x3, cw_ref, cs).reshape(t, CONV_CW) + cb_ref[:, cs]
        act = _silu(acc) * cs_ref[:, cs]
        if (c + 1) * CONV_CW <= SSD_XBC:
            zx_ref[0, :, SSD_WIDTH + c * CONV_CW:SSD_WIDTH + (c + 1) * CONV_CW] = act
        else:
            qk_ref[0, :, c * CONV_CW - SSD_XBC:(c + 1) * CONV_CW - SSD_XBC] = act
    v_ref[0] = _dot(h, w_ref[:, P_V:P_O]).astype(BF16)
    so_ref[0] = _sigmoid(_dot(h, w_ref[:, P_O:P_G]))
    gt_ref[0] = _dot(h, w_ref[:, P_G:D_INP])


def _in_proj(x, g, w, conv_w, conv_b, conv_scale, *, tt, interpret=False):
    b, l, d = x.shape
    tok = lambda width: pl.BlockSpec((1, tt, width), lambda i, j: (i, j, 0))
    const = lambda a: pl.BlockSpec(a.shape, lambda i, j, _n=a.ndim: (0,) * _n)
    sds = lambda width, dt: jax.ShapeDtypeStruct((b, l, width), dt)
    return pl.pallas_call(
        _inproj_kernel,
        grid=(b, l // tt),
        in_specs=[tok(d), const(g), const(w), const(conv_w), const(conv_b), const(conv_scale)],
        out_specs=[tok(S5_WIDTH), tok(SSD_WIDTH + SSD_XBC), tok(2 * MLSTM_QK), tok(MLSTM_WIDTH),
                   tok(MLSTM_WIDTH), tok(LANES)],
        out_shape=[sds(S5_WIDTH, F32), sds(SSD_WIDTH + SSD_XBC, F32), sds(2 * MLSTM_QK, F32),
                   sds(MLSTM_WIDTH, BF16), sds(MLSTM_WIDTH, F32), sds(LANES, F32)],
        scratch_shapes=[pltpu.VMEM((HALO, CONV_W), F32)],
        compiler_params=pltpu.CompilerParams(
            dimension_semantics=("parallel", "arbitrary"), vmem_limit_bytes=VMEM_LIMIT),
        name="in_proj",
        interpret=interpret,
    )(x, g, w, conv_w, conv_b, conv_scale)


S5_KB = 2
S5_KW = S5_WIDTH // S5_KB
S5_HALF = S5_NSTATE // S5_KB
S5_STRIP = 256


def _s5_kernel(u_ref, bblk_ref, cblk_ref, are_ref, aim_ref, d_ref, wglu_ref, bglu_ref,
               y_ref, bu_ref, xs_ref, st_ref, *, nb, tc):
    @pl.when(pl.program_id(0) == 0)
    def _():
        st_ref[...] = jnp.zeros_like(st_ref)

    m = nb * tc
    u = pltpu.einshape("btd->tbd", u_ref[...]).reshape(m, S5_WIDTH)
    ub = u.astype(BF16)
    for kb in range(S5_KB):
        bu_ref[:, kb * 2 * S5_HALF:(kb + 1) * 2 * S5_HALF] = _dot(
            ub[:, kb * S5_KW:(kb + 1) * S5_KW], bblk_ref[kb])

    for kb in range(S5_KB):
        for s in range(S5_HALF // S5_STRIP):
            c_re = kb * 2 * S5_HALF + s * S5_STRIP
            c_im = c_re + S5_HALF
            c_a = kb * S5_HALF + s * S5_STRIP
            ar = jnp.broadcast_to(are_ref[:, c_a:c_a + S5_STRIP], (nb, S5_STRIP))
            ai = jnp.broadcast_to(aim_ref[:, c_a:c_a + S5_STRIP], (nb, S5_STRIP))

            def step(t, carry, c_re=c_re, c_im=c_im, ar=ar, ai=ai):
                xr, xi = carry
                r0 = pl.multiple_of(t * nb, nb)
                br = bu_ref[pl.ds(r0, nb), c_re:c_re + S5_STRIP]
                bi = bu_ref[pl.ds(r0, nb), c_im:c_im + S5_STRIP]
                nr = ar * xr - ai * xi + br
                ni = ar * xi + ai * xr + bi
                xs_ref[pl.ds(r0, nb), c_re:c_re + S5_STRIP] = nr.astype(BF16)
                xs_ref[pl.ds(r0, nb), c_im:c_im + S5_STRIP] = ni.astype(BF16)
                return nr, ni

            xr, xi = lax.fori_loop(
                0, tc, step,
                (st_ref[:, c_re:c_re + S5_STRIP], st_ref[:, c_im:c_im + S5_STRIP]), unroll=4)
            st_ref[:, c_re:c_re + S5_STRIP] = xr
            st_ref[:, c_im:c_im + S5_STRIP] = xi

    ys = [_dot(xs_ref[:, kb * 2 * S5_HALF:(kb + 1) * 2 * S5_HALF], cblk_ref[kb])
          for kb in range(S5_KB)]
    y = jnp.concatenate(ys, axis=1) + d_ref[...] * u
    g = jax.nn.gelu(y)
    out = g * _sigmoid(_dot(g.astype(BF16), wglu_ref[...]) + bglu_ref[...])
    y_ref[...] = pltpu.einshape("tbd->btd", out.reshape(tc, nb, S5_WIDTH)).astype(BF16)


def _s5(u, bblk, cblk, a_re, a_im, d_row, w_glu, b_glu, *, tc, interpret=False):
    nb, l, _ = u.shape
    m = nb * tc
    full = lambda a: pl.BlockSpec(a.shape, lambda i, _n=a.ndim: (0,) * _n)
    blk = pl.BlockSpec((nb, tc, S5_WIDTH), lambda i: (0, i, 0))
    return pl.pallas_call(
        functools.partial(_s5_kernel, nb=nb, tc=tc),
        grid=(l // tc,),
        in_specs=[blk, full(bblk), full(cblk), full(a_re), full(a_im), full(d_row),
                  full(w_glu), full(b_glu)],
        out_specs=blk,
        out_shape=jax.ShapeDtypeStruct((nb, l, S5_WIDTH), BF16),
        scratch_shapes=[pltpu.VMEM((m, 2 * S5_NSTATE), F32),
                        pltpu.VMEM((m, 2 * S5_NSTATE), BF16),
                        pltpu.VMEM((nb, 2 * S5_NSTATE), F32)],
        compiler_params=pltpu.CompilerParams(
            dimension_semantics=("arbitrary",), vmem_limit_bytes=VMEM_LIMIT),
        name="s5",
        interpret=interpret,
    )(u, bblk, cblk, a_re, a_im, d_row, w_glu, b_glu)


def _s5_prepare(lam_re, lam_im, log_step, b_re, b_im, c_re, c_im):
    step = jnp.exp(log_step)[:, None]
    mag = jnp.exp(lam_re * step)
    ang = lam_im * step
    a_re = mag * jnp.cos(ang)
    a_im = mag * jnp.sin(ang)
    den = lam_re * lam_re + lam_im * lam_im
    nr = a_re - 1.0
    f_re = (nr * lam_re + a_im * lam_im) / den
    f_im = (a_im * lam_re - nr * lam_im) / den
    bb_re = f_re[..., None] * b_re - f_im[..., None] * b_im
    bb_im = f_re[..., None] * b_im + f_im[..., None] * b_re
    gl = S5_GROUPS // S5_KB
    eye = jnp.eye(gl, dtype=F32)

    def b_block(bb):
        bb = bb.reshape(S5_KB, gl, S5_STATE, S5_GROUP)
        blk = jnp.einsum('kgph,gq->kghqp', bb, eye)
        return blk.reshape(S5_KB, gl * S5_GROUP, gl * S5_STATE)

    def c_block(cc):
        cc = cc.reshape(S5_KB, gl, S5_GROUP, S5_STATE)
        blk = jnp.einsum('kghp,gq->kgpqh', cc, eye)
        return blk.reshape(S5_KB, gl * S5_STATE, gl * S5_GROUP)

    bblk = jnp.concatenate([b_block(bb_re), b_block(bb_im)], axis=2).astype(BF16)
    cblk = jnp.concatenate([c_block(c_re), -c_block(c_im)], axis=1).astype(BF16)
    return bblk, cblk, a_re.reshape(1, S5_NSTATE), a_im.reshape(1, S5_NSTATE)


def _ssd_kernel(zx_ref, gt_ref, dtb_ref, alog_ref, dsk_ref, ng_ref, y_ref, hst):
    @pl.when(pl.program_id(1) == 0)
    def _():
        hst[...] = jnp.zeros_like(hst)

    for sq in range(zx_ref.shape[0]):
        _ssd_chunk(zx_ref.at[sq], gt_ref.at[sq], dtb_ref, alog_ref, dsk_ref, ng_ref,
                   y_ref.at[sq], hst.at[sq])


def _ssd_chunk(zx_ref, gt_ref, dtb_ref, alog_ref, dsk_ref, ng_ref, y_ref, hst):
    t = CHUNK
    xs = zx_ref[:, SSD_WIDTH:2 * SSD_WIDTH]
    dt = _softplus(gt_ref[...] + dtb_ref[...])
    acs = _cumsum_rows(dt * (-jnp.exp(alog_ref[...]))) * LOG2E
    acs_t = acs.T
    dt_e = _expand_heads(dt, GATE_DT, SSD_HEADS, SSD_HEADDIM)
    acs_e = _expand_heads(acs, GATE_DT, SSD_HEADS, SSD_HEADDIM)
    last_e = acs_e[t - 1:t, :]
    xdt = xs * dt_e
    xdt_b = xdt.astype(BF16)
    xw_b = (xdt * jnp.exp2(last_e - acs_e)).astype(BF16)
    eacs_e = jnp.exp2(acs_e)
    cdec_e = jnp.exp2(last_e)

    row = lax.broadcasted_iota(jnp.int32, (t, t), 0)
    col = lax.broadcasted_iota(jnp.int32, (t, t), 1)
    causal = col <= row
    masks = _half_masks(t, SSD_HEADDIM)
    ys = []
    for g in range(SSD_GROUPS):
        b0 = 2 * SSD_WIDTH + g * SSD_STATE
        c0 = 2 * SSD_WIDTH + SSD_GROUPS * SSD_STATE + g * SSD_STATE
        bm = zx_ref[:, b0:b0 + SSD_STATE].astype(BF16)
        cm = zx_ref[:, c0:c0 + SSD_STATE].astype(BF16)
        cbm = _dot_nt(cm, bm)
        gs = slice(g * SSD_GW, (g + 1) * SSD_GW)
        h_prev = hst[g]
        y_g = _dot(cm, h_prev.astype(BF16)) * eacs_e[:, gs]
        diag = []
        for pr in range(SSD_HPG // 2):
            ps = slice(g * SSD_GW + pr * LANES, g * SSD_GW + (pr + 1) * LANES)
            acc = None
            for q in range(2):
                j = g * SSD_HPG + pr * 2 + q
                seg = acs[:, j:j + 1] - acs_t[j:j + 1, :]
                m = (cbm * jnp.exp2(jnp.where(causal, seg, -jnp.inf))).astype(BF16)
                part = _dot(m, xdt_b[:, ps] * masks[q])
                acc = part if acc is None else acc + part
            diag.append(acc)
        ys.append(y_g + jnp.concatenate(diag, axis=1))
        hst[g] = h_prev * cdec_e[:, gs] + _dot_tn(bm, xw_b[:, gs])
    y = jnp.concatenate(ys, axis=1) + xs * dsk_ref[...]
    y = y * zx_ref[:, 0:SSD_WIDTH]
    outs = []
    for g in range(SSD_GROUPS):
        gs = slice(g * SSD_GW, (g + 1) * SSD_GW)
        outs.append(_rms(y[:, gs], ng_ref[:, gs]))
    y_ref[...] = jnp.concatenate(outs, axis=1).astype(BF16)


def _ssd(zx, gates, dtb, alog, dsk, ng, *, nseq, interpret=False):
    b, l, _ = zx.shape
    full = lambda a: pl.BlockSpec(a.shape, lambda i, j, _n=a.ndim: (0,) * _n)
    tok = lambda width: pl.BlockSpec((nseq, CHUNK, width), lambda i, j: (i, j, 0))
    return pl.pallas_call(
        _ssd_kernel,
        grid=(b // nseq, l // CHUNK),
        in_specs=[tok(SSD_WIDTH + SSD_XBC), tok(LANES), full(dtb), full(alog), full(dsk), full(ng)],
        out_specs=tok(SSD_WIDTH),
        out_shape=jax.ShapeDtypeStruct((b, l, SSD_WIDTH), BF16),
        scratch_shapes=[pltpu.VMEM((nseq, SSD_GROUPS, SSD_STATE, SSD_GW), F32)],
        compiler_params=pltpu.CompilerParams(
            dimension_semantics=("parallel", "arbitrary"), vmem_limit_bytes=VMEM_LIMIT),
        name="ssd",
        interpret=interpret,
    )(zx, gates, dtb, alog, dsk, ng)


def _mlstm_kernel(qk_ref, v_ref, so_ref, gt_ref, bi_ref, bf_ref, y_ref, cst, nst, mst):
    @pl.when(pl.program_id(1) == 0)
    def _():
        cst[...] = jnp.zeros_like(cst)
        nst[...] = jnp.zeros_like(nst)
        mst[...] = jnp.zeros_like(mst)

    for sq in range(qk_ref.shape[0]):
        _mlstm_chunk(qk_ref.at[sq], v_ref.at[sq], so_ref.at[sq], gt_ref.at[sq], bi_ref, bf_ref,
                     y_ref.at[sq], cst.at[sq], nst.at[sq], mst.at[sq])


def _mlstm_chunk(qk_ref, v_ref, so_ref, gt_ref, bi_ref, bf_ref, y_ref, cst, nst, mst):
    t = CHUNK
    gates = gt_ref[...]
    log_i = gates + bi_ref[...]
    log_f = -_softplus(-(gates + bf_ref[...]))
    log_f = pltpu.roll(log_f, LANES - (GATE_F - GATE_I), axis=1)
    bcum = _cumsum_rows(log_f)
    g_tot = bcum[t - 1:t, :]
    gvec = log_i - bcum
    cmax = _cummax_rows(gvec)
    gmax = cmax[t - 1:t, :]
    m_prev = mst[...]
    w_inter = bcum + m_prev
    m_row = jnp.maximum(w_inter, bcum + cmax)
    alpha2 = (bcum - m_row) * LOG2E
    gvec2_t = (gvec * LOG2E).T
    s_int = jnp.exp(w_inter - m_row)
    enm = jnp.exp(-m_row)
    a_loc = g_tot + gmax
    e_end = jnp.exp(gvec - gmax)
    m_new = jnp.maximum(g_tot + m_prev, a_loc)
    s_old = jnp.exp(g_tot + m_prev - m_new)
    s_new = jnp.exp(a_loc - m_new)
    mst[...] = m_new

    q = qk_ref[:, 0:MLSTM_QK]
    k = qk_ref[:, MLSTM_QK:]
    qb = q.astype(BF16)
    kb = k.astype(BF16)
    n_prev = nst[...]
    sqn = s_int * _dot(qb, n_prev.astype(BF16))
    nrow = lax.broadcasted_iota(jnp.int32, (MLSTM_QK, LANES), 0)
    nlane = lax.broadcasted_iota(jnp.int32, (MLSTM_QK, LANES), 1)
    own = (nlane >= GATE_I) & ((nlane - GATE_I) * MLSTM_QKDIM <= nrow) & (
        nrow < (nlane - GATE_I + 1) * MLSTM_QKDIM)
    n_loc = jnp.where(own, _dot_tn(kb, e_end.astype(BF16)), 0.0)
    nst[...] = s_old * n_prev + s_new * n_loc
    kwb = (k * _expand_heads(e_end, GATE_I, MLSTM_HEADS, MLSTM_QKDIM)).astype(BF16)

    row = lax.broadcasted_iota(jnp.int32, (t, t), 0)
    col = lax.broadcasted_iota(jnp.int32, (t, t), 1)
    causal = col <= row
    masks = _half_masks(t, MLSTM_QKDIM)
    srow = lax.broadcasted_iota(jnp.int32, (LANES, MLSTM_VDIM), 0)
    for pr in range(MLSTM_HEADS // 2):
        ps = slice(pr * LANES, (pr + 1) * LANES)
        c_prev = cst[pr]
        c_prev_b = c_prev.astype(BF16)
        upd = []
        for hq in range(2):
            h = pr * 2 + hq
            ln = GATE_I + h
            hs = slice(h * MLSTM_VDIM, (h + 1) * MLSTM_VDIM)
            q_hb = qb[:, ps] * masks[hq]
            v_h = v_ref[:, hs]
            s = _dot_nt(q_hb, kb[:, ps])
            dm = jnp.where(causal, alpha2[:, ln:ln + 1] + gvec2_t[ln:ln + 1, :], -jnp.inf)
            qs = s * jnp.exp2(dm)
            num = _dot(qs.astype(BF16), v_h) + s_int[:, ln:ln + 1] * _dot(q_hb, c_prev_b)
            den = jnp.sum(qs, axis=1, keepdims=True) + sqn[:, ln:ln + 1]
            den = jnp.maximum(jnp.abs(den), enm[:, ln:ln + 1])
            y_ref[:, hs] = (so_ref[:, hs] * (num / den)).astype(BF16)
            upd.append(_dot_tn(kwb[:, ps], v_h))
        lo = GATE_I + 2 * pr
        so = jnp.where(srow < MLSTM_QKDIM, s_old[:, lo:lo + 1], s_old[:, lo + 1:lo + 2])
        sn = jnp.where(srow < MLSTM_QKDIM, s_new[:, lo:lo + 1], s_new[:, lo + 1:lo + 2])
        cst[pr] = so * c_prev + sn * jnp.where(srow < MLSTM_QKDIM, upd[0], upd[1])


def _mlstm(qk, v, so, gates, bi, bf, *, nseq, interpret=False):
    b, l, _ = qk.shape
    full = lambda a: pl.BlockSpec(a.shape, lambda i, j, _n=a.ndim: (0,) * _n)
    tok = lambda width: pl.BlockSpec((nseq, CHUNK, width), lambda i, j: (i, j, 0))
    return pl.pallas_call(
        _mlstm_kernel,
        grid=(b // nseq, l // CHUNK),
        in_specs=[tok(2 * MLSTM_QK), tok(MLSTM_WIDTH), tok(MLSTM_WIDTH), tok(LANES),
                  full(bi), full(bf)],
        out_specs=tok(MLSTM_WIDTH),
        out_shape=jax.ShapeDtypeStruct((b, l, MLSTM_WIDTH), BF16),
        scratch_shapes=[pltpu.VMEM((nseq, MLSTM_HEADS // 2, LANES, MLSTM_VDIM), F32),
                        pltpu.VMEM((nseq, MLSTM_QK, LANES), F32),
                        pltpu.VMEM((nseq, 1, LANES), F32)],
        compiler_params=pltpu.CompilerParams(
            dimension_semantics=("parallel", "arbitrary"), vmem_limit_bytes=VMEM_LIMIT),
        name="mlstm",
        interpret=interpret,
    )(qk, v, so, gates, bi, bf)


def _outffn_kernel(x_ref, ya_ref, yb_ref, yc_ref, wo_ref, g1_ref, g2_ref, wg_ref, wu_ref,
                   wd_ref, g3_ref, o_ref):
    a0 = S5_WIDTH
    a1 = S5_WIDTH + SSD_WIDTH
    mix = (_dot(ya_ref[0], wo_ref[0:a0, :]) + _dot(yb_ref[0], wo_ref[a0:a1, :])
           + _dot(yc_ref[0], wo_ref[a1:, :]))
    x1 = x_ref[0] + _rms(mix, g1_ref[...])
    h = _rms(x1, g2_ref[...]).astype(BF16)
    act = (_silu(_dot(h, wg_ref[...])) * _dot(h, wu_ref[...])).astype(BF16)
    ff = _dot(act, wd_ref[...])
    o_ref[0] = x1 + _rms(ff, g3_ref[...])


def _out_ffn(x, ya, yb, yc, wo, g1, g2, wg, wu, wd, g3, *, tt, interpret=False):
    b, l, d = x.shape
    tok = lambda width: pl.BlockSpec((1, tt, width), lambda i, j: (i, j, 0))
    const = lambda a: pl.BlockSpec(a.shape, lambda i, j, _n=a.ndim: (0,) * _n,
                                   pipeline_mode=pl.Buffered(1))
    return pl.pallas_call(
        _outffn_kernel,
        grid=(b, l // tt),
        in_specs=[tok(d), tok(S5_WIDTH), tok(SSD_WIDTH), tok(MLSTM_WIDTH), const(wo), const(g1),
                  const(g2), const(wg), const(wu), const(wd), const(g3)],
        out_specs=tok(d),
        out_shape=jax.ShapeDtypeStruct((b, l, d), F32),
        compiler_params=pltpu.CompilerParams(
            dimension_semantics=("parallel", "parallel"), vmem_limit_bytes=VMEM_LIMIT),
        name="out_ffn",
        interpret=interpret,
    )(x, ya, yb, yc, wo, g1, g2, wg, wu, wd, g3)


def _lane_row(vals, offset):
    return jnp.zeros((1, LANES), F32).at[0, offset:offset + vals.shape[0]].set(vals)


def _pack_w_in(w_in):
    offs = [0]
    for s in IN_SPLIT_SIZES:
        offs.append(offs[-1] + s)
    seg = lambda i: w_in[..., offs[i]:offs[i + 1]]
    u, z, xbc, dt, qk, v, gi, gf, o = (seg(i) for i in range(9))
    pad = jnp.zeros(w_in.shape[:-1] + (LANES - SSD_HEADS - 2 * MLSTM_HEADS,), w_in.dtype)
    return jnp.concatenate([u, z, xbc, qk, v, o, dt, gi, gf, pad], axis=-1).astype(BF16)


def _layer(x, p, *, tt_in, tt_out, tc, nseq, interpret):
    u, zx, qk, v, so, gates = _in_proj(x, p["norm_pre_mix"], p["w_in"], p["conv_w"], p["conv_b"],
                                       p["conv_scale"], tt=tt_in, interpret=interpret)
    ya = _s5(u, p["s5_bblk"], p["s5_cblk"], p["s5_are"], p["s5_aim"],
             p["s5_d"], p["s5_w_glu"], p["s5_b_glu"], tc=tc, interpret=interpret)
    yb = _ssd(zx, gates, p["ssd_dtb"], p["ssd_alog"], p["ssd_dsk"], p["ssd_norm"],
              nseq=nseq, interpret=interpret)
    yc = _mlstm(qk, v, so, gates, p["mlstm_bi"], p["mlstm_bf"], nseq=nseq,
                interpret=interpret)
    return _out_ffn(x, ya, yb, yc, p["w_out"], p["norm_post_mix"],
                    p["norm_pre_ffn"], p["w_gate"], p["w_up"], p["w_down"], p["norm_post_ffn"],
                    tt=tt_out, interpret=interpret)


def _prepare_params(norm_pre_mix, w_in, s5_lambda_re, s5_lambda_im, s5_log_step, s5_b_re, s5_b_im,
                    s5_c_re, s5_c_im, s5_d, s5_w_glu, s5_b_glu, ssd_conv_w, ssd_conv_b,
                    ssd_dt_bias, ssd_a_log, ssd_d, ssd_norm, mlstm_conv_w, mlstm_conv_b, mlstm_b_i,
                    mlstm_b_f, w_out, norm_post_mix, norm_pre_ffn, w_gate, w_up, w_down,
                    norm_post_ffn):
    depth = w_in.shape[0]
    row = lambda a: a.reshape(depth, 1, -1).astype(F32)
    bblk, cblk, a_re, a_im = jax.vmap(_s5_prepare)(
        s5_lambda_re.astype(F32), s5_lambda_im.astype(F32), s5_log_step.astype(F32),
        s5_b_re.astype(F32), s5_b_im.astype(F32), s5_c_re.astype(F32), s5_c_im.astype(F32))
    lanes = lambda a, off: jax.vmap(lambda v: _lane_row(v.astype(F32), off))(a)
    conv_scale = jnp.concatenate(
        [jnp.ones((SSD_XBC + MLSTM_QK,), F32), jnp.full((MLSTM_QK,), MLSTM_QKDIM ** -0.5, F32)])
    return dict(
        norm_pre_mix=row(norm_pre_mix), w_in=_pack_w_in(w_in),
        conv_w=jnp.concatenate([ssd_conv_w, mlstm_conv_w], axis=-1).astype(F32),
        conv_b=row(jnp.concatenate([ssd_conv_b, mlstm_conv_b], axis=-1)),
        conv_scale=jnp.broadcast_to(conv_scale, (depth, 1, CONV_W)),
        s5_bblk=bblk, s5_cblk=cblk, s5_are=a_re, s5_aim=a_im, s5_d=row(s5_d),
        s5_w_glu=s5_w_glu.astype(BF16), s5_b_glu=row(s5_b_glu),
        ssd_dtb=lanes(ssd_dt_bias, GATE_DT), ssd_alog=lanes(ssd_a_log, GATE_DT),
        ssd_dsk=row(jnp.repeat(ssd_d, SSD_HEADDIM, axis=-1)), ssd_norm=row(ssd_norm),
        mlstm_bi=lanes(mlstm_b_i, GATE_I), mlstm_bf=lanes(mlstm_b_f, GATE_F),
        w_out=w_out.astype(BF16), norm_post_mix=row(norm_post_mix),
        norm_pre_ffn=row(norm_pre_ffn), w_gate=w_gate.astype(BF16), w_up=w_up.astype(BF16),
        w_down=w_down.astype(BF16), norm_post_ffn=row(norm_post_ffn))


def _forward(x, *params, tt_in=256, tt_out=512, tc=32, nseq=2, interpret=False):
    stacked = _prepare_params(*params)

    def body(xc, p):
        return _layer(xc, p, tt_in=tt_in, tt_out=tt_out, tc=tc, nseq=nseq,
                      interpret=interpret), None

    out, _ = lax.scan(body, x, stacked)
    return out


def kernel(x, norm_pre_mix, w_in, s5_lambda_re, s5_lambda_im, s5_log_step, s5_b_re, s5_b_im, s5_c_re, s5_c_im, s5_d, s5_w_glu, s5_b_glu, ssd_conv_w, ssd_conv_b, ssd_dt_bias, ssd_a_log, ssd_d, ssd_norm, mlstm_conv_w, mlstm_conv_b, mlstm_b_i, mlstm_b_f, w_out, norm_post_mix, norm_pre_ffn, w_gate, w_up, w_down, norm_post_ffn):
    return _forward(x, norm_pre_mix, w_in, s5_lambda_re, s5_lambda_im, s5_log_step, s5_b_re,
                    s5_b_im, s5_c_re, s5_c_im, s5_d, s5_w_glu, s5_b_glu, ssd_conv_w, ssd_conv_b,
                    ssd_dt_bias, ssd_a_log, ssd_d, ssd_norm, mlstm_conv_w, mlstm_conv_b, mlstm_b_i,
                    mlstm_b_f, w_out, norm_post_mix, norm_pre_ffn, w_gate, w_up, w_down,
                    norm_post_ffn)
```

```python
import functools
import math

import jax
import jax.numpy as jnp
from jax import lax
from jax.experimental import pallas as pl
from jax.experimental.pallas import tpu as pltpu

F32 = jnp.float32
BF16 = jnp.bfloat16

D_MODEL = 1024
D_MIX = 2048
S5_WIDTH = 512
S5_GROUP = 16
S5_GROUPS = 32
S5_STATE = 64
S5_NSTATE = S5_GROUPS * S5_STATE
SSD_HEADDIM = 64
SSD_WIDTH = 768
SSD_HEADS = 12
SSD_GROUPS = 2
SSD_HPG = SSD_HEADS // SSD_GROUPS
SSD_GW = SSD_WIDTH // SSD_GROUPS
SSD_STATE = 128
SSD_XBC = SSD_WIDTH + 2 * SSD_GROUPS * SSD_STATE
MLSTM_WIDTH = 768
MLSTM_HEADS = 6
MLSTM_VDIM = 128
MLSTM_QKDIM = 64
MLSTM_QK = 384
CONV_K = 4
D_FF = 2816
NORM_EPS = 1e-6
IN_SPLIT_SIZES = (S5_WIDTH, SSD_WIDTH, SSD_XBC, SSD_HEADS, 2 * MLSTM_QK, MLSTM_WIDTH,
                  MLSTM_HEADS, MLSTM_HEADS, MLSTM_WIDTH)

LANES = 128
CHUNK = 128
HALO = 8
LOG2E = math.log2(math.e)
GATE_DT = 0
GATE_I = SSD_HEADS
GATE_F = SSD_HEADS + MLSTM_HEADS
P_U = 0
P_Z = P_U + S5_WIDTH
P_CONV = P_Z + SSD_WIDTH
P_V = P_CONV + SSD_XBC + 2 * MLSTM_QK
P_O = P_V + MLSTM_WIDTH
P_G = P_O + MLSTM_WIDTH
D_INP = P_G + LANES
CONV_W = SSD_XBC + 2 * MLSTM_QK

VMEM_LIMIT = 56 * 1024 * 1024


def _silu(x):
    hx = 0.5 * x
    return hx + hx * jnp.tanh(hx)


def _sigmoid(x):
    return 0.5 + 0.5 * jnp.tanh(0.5 * x)


def _softplus(x):
    return jnp.maximum(x, 0.0) + jnp.log1p(jnp.exp(-jnp.abs(x)))


def _rms(x, g):
    return x * lax.rsqrt(jnp.mean(x * x, axis=-1, keepdims=True) + NORM_EPS) * g


def _dot(a, b):
    return jnp.dot(a, b, preferred_element_type=F32)


def _dot_nt(a, b):
    return lax.dot_general(a, b, (((1,), (1,)), ((), ())), preferred_element_type=F32)


def _dot_tn(a, b):
    return lax.dot_general(a, b, (((0,), (0,)), ((), ())), preferred_element_type=F32)


def _cumsum_rows(x):
    t = x.shape[0]
    r = lax.broadcasted_iota(jnp.int32, (t, t), 0)
    c = lax.broadcasted_iota(jnp.int32, (t, t), 1)
    tril = jnp.where(c <= r, 1.0, 0.0).astype(F32)
    return jnp.dot(tril, x, precision=lax.Precision.HIGHEST, preferred_element_type=F32)


def _cummax_rows(x):
    t = x.shape[0]
    row = lax.broadcasted_iota(jnp.int32, x.shape, 0)
    d = 1
    while d < t:
        x = jnp.maximum(x, jnp.where(row >= d, pltpu.roll(x, d, axis=0), -jnp.inf))
        d *= 2
    return x


def _split3(x):
    hi = x.astype(BF16)
    r1 = x - hi.astype(F32)
    mid = r1.astype(BF16)
    lo = (r1 - mid.astype(F32)).astype(BF16)
    return hi, mid, lo


def _expand_matrix(first, heads, width):
    src = jnp.arange(LANES)[:, None] - first
    dst = jnp.arange(heads * width)[None, :]
    sel = ((dst >= src * width) & (dst < (src + 1) * width)).astype(BF16)
    return jnp.concatenate([sel, sel, sel], axis=0)


def _expand_heads(v, sel3_ref):
    return _dot(jnp.concatenate(_split3(v), axis=1), sel3_ref[...])


def _half_masks(t, width):
    lane = lax.broadcasted_iota(jnp.int32, (t, LANES), 1)
    lo = jnp.where(lane < width, 1.0, 0.0).astype(BF16)
    hi = jnp.where(lane >= width, 1.0, 0.0).astype(BF16)
    return lo, hi


CONV_CW = 256


def _shift_rows(x3, prev3, j):
    rolled = pltpu.roll(jnp.concatenate([prev3, x3], axis=0), j, axis=1)
    sub = lax.broadcasted_iota(jnp.int32, x3.shape, 1)
    return jnp.where(sub >= j, rolled[1:], rolled[:-1])


def _causal_conv4(x3, w_ref, cs):
    zero = jnp.zeros((1,) + x3.shape[1:], F32)
    x1 = _shift_rows(x3, zero, 1)
    a = w_ref[3:4, cs] * x3[1:] + w_ref[2:3, cs] * x1[1:]
    b = w_ref[1:2, cs] * x3 + w_ref[0:1, cs] * x1
    return a + _shift_rows(b[1:], b[0:1], 2)


def _inproj_kernel(x_ref, g_ref, w_ref, cw_ref, cb_ref, cs_ref,
                   u_ref, zx_ref, qk_ref, v_ref, so_ref, gt_ref, carry):
    t = x_ref.shape[1]

    @pl.when(pl.program_id(1) == 0)
    def _():
        carry[...] = jnp.zeros_like(carry)

    h = _rms(x_ref[0], g_ref[...]).astype(BF16)
    u_ref[0] = _dot(h, w_ref[:, P_U:P_Z])
    zx_ref[0, :, 0:SSD_WIDTH] = _silu(_dot(h, w_ref[:, P_Z:P_CONV]))
    for c in range(CONV_W // CONV_CW):
        cs = slice(c * CONV_CW, (c + 1) * CONV_CW)
        pc = _dot(h, w_ref[:, P_CONV + c * CONV_CW:P_CONV + (c + 1) * CONV_CW])
        x3 = jnp.concatenate([carry[:, cs], pc], axis=0).reshape(t // HALO + 1, HALO, CONV_CW)
        carry[:, cs] = pc[t - HALO:, :]
        acc = _causal_conv4(x3, cw_ref, cs).reshape(t, CONV_CW) + cb_ref[:, cs]
        act = _silu(acc) * cs_ref[:, cs]
        if (c + 1) * CONV_CW <= SSD_XBC:
            zx_ref[0, :, SSD_WIDTH + c * CONV_CW:SSD_WIDTH + (c + 1) * CONV_CW] = act
        else:
            qk_ref[0, :, c * CONV_CW - SSD_XBC:(c + 1) * CONV_CW - SSD_XBC] = act
    v_ref[0] = _dot(h, w_ref[:, P_V:P_O]).astype(BF16)
    so_ref[0] = _sigmoid(_dot(h, w_ref[:, P_O:P_G]))
    gt_ref[0] = _dot(h, w_ref[:, P_G:D_INP])


def _in_proj(x, g, w, conv_w, conv_b, conv_scale, *, tt, interpret=False):
    b, l, d = x.shape
    tok = lambda width: pl.BlockSpec((1, tt, width), lambda i, j: (i, j, 0))
    const = lambda a: pl.BlockSpec(a.shape, lambda i, j, _n=a.ndim: (0,) * _n)
    sds = lambda width, dt: jax.ShapeDtypeStruct((b, l, width), dt)
    return pl.pallas_call(
        _inproj_kernel,
        grid=(b, l // tt),
        in_specs=[tok(d), const(g), const(w), const(conv_w), const(conv_b), const(conv_scale)],
        out_specs=[tok(S5_WIDTH), tok(SSD_WIDTH + SSD_XBC), tok(2 * MLSTM_QK), tok(MLSTM_WIDTH),
                   tok(MLSTM_WIDTH), tok(LANES)],
        out_shape=[sds(S5_WIDTH, F32), sds(SSD_WIDTH + SSD_XBC, F32), sds(2 * MLSTM_QK, F32),
                   sds(MLSTM_WIDTH, BF16), sds(MLSTM_WIDTH, F32), sds(LANES, F32)],
        scratch_shapes=[pltpu.VMEM((HALO, CONV_W), F32)],
        compiler_params=pltpu.CompilerParams(
            dimension_semantics=("parallel", "arbitrary"), vmem_limit_bytes=VMEM_LIMIT),
        name="in_proj",
        interpret=interpret,
    )(x, g, w, conv_w, conv_b, conv_scale)


S5_KB = 2
S5_KW = S5_WIDTH // S5_KB
S5_HALF = S5_NSTATE // S5_KB
S5_STRIP = 256


def _s5_kernel(u_ref, bblk_ref, cblk_ref, are_ref, aim_ref, d_ref, wglu_ref, bglu_ref,
               y_ref, bu_ref, xs_ref, st_ref, *, nb, tc):
    @pl.when(pl.program_id(0) == 0)
    def _():
        st_ref[...] = jnp.zeros_like(st_ref)

    m = nb * tc
    u = pltpu.einshape("btd->tbd", u_ref[...]).reshape(m, S5_WIDTH)
    ub = u.astype(BF16)
    ys = []
    for kb in range(S5_KB):
        ukb = ub[:, kb * S5_KW:(kb + 1) * S5_KW]
        ykb = None
        for s in range(S5_HALF // S5_STRIP):
            c_re = kb * 2 * S5_HALF + s * S5_STRIP
            c_im = c_re + S5_HALF
            c_a = kb * S5_HALF + s * S5_STRIP
            re_cols = slice(c_re, c_re + S5_STRIP)
            im_cols = slice(c_im, c_im + S5_STRIP)
            w_re = slice(s * S5_STRIP, (s + 1) * S5_STRIP)
            w_im = slice(S5_HALF + s * S5_STRIP, S5_HALF + (s + 1) * S5_STRIP)
            bu_ref[:, re_cols] = _dot(ukb, bblk_ref[kb, :, w_re])
            bu_ref[:, im_cols] = _dot(ukb, bblk_ref[kb, :, w_im])
            ar = jnp.broadcast_to(are_ref[:, c_a:c_a + S5_STRIP], (nb, S5_STRIP))
            ai = jnp.broadcast_to(aim_ref[:, c_a:c_a + S5_STRIP], (nb, S5_STRIP))
            xr = st_ref[:, re_cols]
            xi = st_ref[:, im_cols]
            for t in range(tc):
                rows = slice(t * nb, (t + 1) * nb)
                xr, xi = (ar * xr - ai * xi + bu_ref[rows, re_cols],
                          ar * xi + ai * xr + bu_ref[rows, im_cols])
                xs_ref[rows, re_cols] = xr.astype(BF16)
                xs_ref[rows, im_cols] = xi.astype(BF16)
            st_ref[:, re_cols] = xr
            st_ref[:, im_cols] = xi
            part = (_dot(xs_ref[:, re_cols], cblk_ref[kb, w_re, :])
                    + _dot(xs_ref[:, im_cols], cblk_ref[kb, w_im, :]))
            ykb = part if ykb is None else ykb + part
        ys.append(ykb)
    y = jnp.concatenate(ys, axis=1) + d_ref[...] * u
    g = jax.nn.gelu(y)
    out = g * _sigmoid(_dot(g.astype(BF16), wglu_ref[...]) + bglu_ref[...])
    y_ref[...] = pltpu.einshape("tbd->btd", out.reshape(tc, nb, S5_WIDTH)).astype(BF16)


def _s5(u, bblk, cblk, a_re, a_im, d_row, w_glu, b_glu, *, tc, interpret=False):
    nb, l, _ = u.shape
    m = nb * tc
    full = lambda a: pl.BlockSpec(a.shape, lambda i, _n=a.ndim: (0,) * _n)
    blk = pl.BlockSpec((nb, tc, S5_WIDTH), lambda i: (0, i, 0))
    return pl.pallas_call(
        functools.partial(_s5_kernel, nb=nb, tc=tc),
        grid=(l // tc,),
        in_specs=[blk, full(bblk), full(cblk), full(a_re), full(a_im), full(d_row),
                  full(w_glu), full(b_glu)],
        out_specs=blk,
        out_shape=jax.ShapeDtypeStruct((nb, l, S5_WIDTH), BF16),
        scratch_shapes=[pltpu.VMEM((m, 2 * S5_NSTATE), F32),
                        pltpu.VMEM((m, 2 * S5_NSTATE), BF16),
                        pltpu.VMEM((nb, 2 * S5_NSTATE), F32)],
        compiler_params=pltpu.CompilerParams(
            dimension_semantics=("arbitrary",), vmem_limit_bytes=VMEM_LIMIT),
        name="s5",
        interpret=interpret,
    )(u, bblk, cblk, a_re, a_im, d_row, w_glu, b_glu)


def _s5_prepare(lam_re, lam_im, log_step, b_re, b_im, c_re, c_im):
    step = jnp.exp(log_step)[:, None]
    mag = jnp.exp(lam_re * step)
    ang = lam_im * step
    a_re = mag * jnp.cos(ang)
    a_im = mag * jnp.sin(ang)
    den = lam_re * lam_re + lam_im * lam_im
    nr = a_re - 1.0
    f_re = (nr * lam_re + a_im * lam_im) / den
    f_im = (a_im * lam_re - nr * lam_im) / den
    bb_re = f_re[..., None] * b_re - f_im[..., None] * b_im
    bb_im = f_re[..., None] * b_im + f_im[..., None] * b_re
    gl = S5_GROUPS // S5_KB
    eye = jnp.eye(gl, dtype=F32)

    def b_block(bb):
        bb = bb.reshape(S5_KB, gl, S5_STATE, S5_GROUP)
        blk = jnp.einsum('kgph,gq->kghqp', bb, eye)
        return blk.reshape(S5_KB, gl * S5_GROUP, gl * S5_STATE)

    def c_block(cc):
        cc = cc.reshape(S5_KB, gl, S5_GROUP, S5_STATE)
        blk = jnp.einsum('kghp,gq->kgpqh', cc, eye)
        return blk.reshape(S5_KB, gl * S5_STATE, gl * S5_GROUP)

    bblk = jnp.concatenate([b_block(bb_re), b_block(bb_im)], axis=2).astype(BF16)
    cblk = jnp.concatenate([c_block(c_re), -c_block(c_im)], axis=1).astype(BF16)
    return bblk, cblk, a_re.reshape(1, S5_NSTATE), a_im.reshape(1, S5_NSTATE)


def _ssd_kernel(zx_ref, gt_ref, dtb_ref, alog_ref, dsk_ref, ng_ref, sel_ref, y_ref, hst):
    @pl.when(pl.program_id(1) == 0)
    def _():
        hst[...] = jnp.zeros_like(hst)

    for sq in range(zx_ref.shape[0]):
        _ssd_chunk(zx_ref.at[sq], gt_ref.at[sq], dtb_ref, alog_ref, dsk_ref, ng_ref, sel_ref,
                   y_ref.at[sq], hst.at[sq])


def _ssd_chunk(zx_ref, gt_ref, dtb_ref, alog_ref, dsk_ref, ng_ref, sel_ref, y_ref, hst):
    t = CHUNK
    xs = zx_ref[:, SSD_WIDTH:2 * SSD_WIDTH]
    dt = _softplus(gt_ref[...] + dtb_ref[...])
    acs = _cumsum_rows(dt * (-jnp.exp(alog_ref[...]))) * LOG2E
    acs_t = acs.T
    dt_e = _expand_heads(dt, sel_ref)
    acs_e = _expand_heads(acs, sel_ref)
    last_e = acs_e[t - 1:t, :]
    xdt = xs * dt_e
    xdt_b = xdt.astype(BF16)
    xw_b = (xdt * jnp.exp2(last_e - acs_e)).astype(BF16)
    eacs_e = jnp.exp2(acs_e)
    cdec_e = jnp.exp2(last_e)

    row = lax.broadcasted_iota(jnp.int32, (t, t), 0)
    col = lax.broadcasted_iota(jnp.int32, (t, t), 1)
    causal = col <= row
    masks = _half_masks(t, SSD_HEADDIM)
    ys = []
    for g in range(SSD_GROUPS):
        b0 = 2 * SSD_WIDTH + g * SSD_STATE
        c0 = 2 * SSD_WIDTH + SSD_GROUPS * SSD_STATE + g * SSD_STATE
        bm = zx_ref[:, b0:b0 + SSD_STATE].astype(BF16)
        cm = zx_ref[:, c0:c0 + SSD_STATE].astype(BF16)
        cbm = _dot_nt(cm, bm)
        gs = slice(g * SSD_GW, (g + 1) * SSD_GW)
        h_prev = hst[g]
        y_g = _dot(cm, h_prev.astype(BF16)) * eacs_e[:, gs]
        diag = []
        for pr in range(SSD_HPG // 2):
            ps = slice(g * SSD_GW + pr * LANES, g * SSD_GW + (pr + 1) * LANES)
            acc = None
            for q in range(2):
                j = g * SSD_HPG + pr * 2 + q
                seg = acs[:, j:j + 1] - acs_t[j:j + 1, :]
                m = (cbm * jnp.exp2(jnp.where(causal, seg, -jnp.inf))).astype(BF16)
                part = _dot(m, xdt_b[:, ps] * masks[q])
                acc = part if acc is None else acc + part
            diag.append(acc)
        ys.append(y_g + jnp.concatenate(diag, axis=1))
        hst[g] = h_prev * cdec_e[:, gs] + _dot_tn(bm, xw_b[:, gs])
    y = jnp.concatenate(ys, axis=1) + xs * dsk_ref[...]
    y = y * zx_ref[:, 0:SSD_WIDTH]
    outs = []
    for g in range(SSD_GROUPS):
        gs = slice(g * SSD_GW, (g + 1) * SSD_GW)
        outs.append(_rms(y[:, gs], ng_ref[:, gs]))
    y_ref[...] = jnp.concatenate(outs, axis=1).astype(BF16)


def _ssd(zx, gates, dtb, alog, dsk, ng, sel, *, nseq, interpret=False):
    b, l, _ = zx.shape
    full = lambda a: pl.BlockSpec(a.shape, lambda i, j, _n=a.ndim: (0,) * _n)
    tok = lambda width: pl.BlockSpec((nseq, CHUNK, width), lambda i, j: (i, j, 0))
    return pl.pallas_call(
        _ssd_kernel,
        grid=(b // nseq, l // CHUNK),
        in_specs=[tok(SSD_WIDTH + SSD_XBC), tok(LANES), full(dtb), full(alog), full(dsk), full(ng),
                  full(sel)],
        out_specs=tok(SSD_WIDTH),
        out_shape=jax.ShapeDtypeStruct((b, l, SSD_WIDTH), BF16),
        scratch_shapes=[pltpu.VMEM((nseq, SSD_GROUPS, SSD_STATE, SSD_GW), F32)],
        compiler_params=pltpu.CompilerParams(
            dimension_semantics=("parallel", "arbitrary"), vmem_limit_bytes=VMEM_LIMIT),
        name="ssd",
        interpret=interpret,
    )(zx, gates, dtb, alog, dsk, ng, sel)


def _mlstm_kernel(qk_ref, v_ref, so_ref, gt_ref, bi_ref, bf_ref, sel_ref, y_ref, cst, nst, mst):
    @pl.when(pl.program_id(1) == 0)
    def _():
        cst[...] = jnp.zeros_like(cst)
        nst[...] = jnp.zeros_like(nst)
        mst[...] = jnp.zeros_like(mst)

    for sq in range(qk_ref.shape[0]):
        _mlstm_chunk(qk_ref.at[sq], v_ref.at[sq], so_ref.at[sq], gt_ref.at[sq], bi_ref, bf_ref,
                     sel_ref, y_ref.at[sq], cst.at[sq], nst.at[sq], mst.at[sq])


def _mlstm_chunk(qk_ref, v_ref, so_ref, gt_ref, bi_ref, bf_ref, sel_ref, y_ref, cst, nst, mst):
    t = CHUNK
    gates = gt_ref[...]
    log_i = gates + bi_ref[...]
    log_f = -_softplus(-(gates + bf_ref[...]))
    log_f = pltpu.roll(log_f, LANES - (GATE_F - GATE_I), axis=1)
    bcum = _cumsum_rows(log_f)
    g_tot = bcum[t - 1:t, :]
    gvec = log_i - bcum
    cmax = _cummax_rows(gvec)
    gmax = cmax[t - 1:t, :]
    m_prev = mst[...]
    w_inter = bcum + m_prev
    m_row = jnp.maximum(w_inter, bcum + cmax)
    alpha2 = (bcum - m_row) * LOG2E
    gvec2_t = (gvec * LOG2E).T
    s_int = jnp.exp(w_inter - m_row)
    enm = jnp.exp(-m_row)
    a_loc = g_tot + gmax
    e_end = jnp.exp(gvec - gmax)
    m_new = jnp.maximum(g_tot + m_prev, a_loc)
    s_old = jnp.exp(g_tot + m_prev - m_new)
    s_new = jnp.exp(a_loc - m_new)
    mst[...] = m_new

    q = qk_ref[:, 0:MLSTM_QK]
    k = qk_ref[:, MLSTM_QK:]
    qb = q.astype(BF16)
    kb = k.astype(BF16)
    n_prev = nst[...]
    sqn = s_int * _dot(qb, n_prev.astype(BF16))
    nrow = lax.broadcasted_iota(jnp.int32, (MLSTM_QK, LANES), 0)
    nlane = lax.broadcasted_iota(jnp.int32, (MLSTM_QK, LANES), 1)
    own = (nlane >= GATE_I) & ((nlane - GATE_I) * MLSTM_QKDIM <= nrow) & (
        nrow < (nlane - GATE_I + 1) * MLSTM_QKDIM)
    n_loc = jnp.where(own, _dot_tn(kb, e_end.astype(BF16)), 0.0)
    nst[...] = s_old * n_prev + s_new * n_loc
    kwb = (k * _expand_heads(e_end, sel_ref)).astype(BF16)

    row = lax.broadcasted_iota(jnp.int32, (t, t), 0)
    col = lax.broadcasted_iota(jnp.int32, (t, t), 1)
    causal = col <= row
    masks = _half_masks(t, MLSTM_QKDIM)
    srow = lax.broadcasted_iota(jnp.int32, (LANES, MLSTM_VDIM), 0)
    for pr in range(MLSTM_HEADS // 2):
        ps = slice(pr * LANES, (pr + 1) * LANES)
        c_prev = cst[pr]
        c_prev_b = c_prev.astype(BF16)
        upd = []
        for hq in range(2):
            h = pr * 2 + hq
            ln = GATE_I + h
            hs = slice(h * MLSTM_VDIM, (h + 1) * MLSTM_VDIM)
            q_hb = qb[:, ps] * masks[hq]
            v_h = v_ref[:, hs]
            s = _dot_nt(q_hb, kb[:, ps])
            dm = jnp.where(causal, alpha2[:, ln:ln + 1] + gvec2_t[ln:ln + 1, :], -jnp.inf)
            qs = s * jnp.exp2(dm)
            num = _dot(qs.astype(BF16), v_h) + s_int[:, ln:ln + 1] * _dot(q_hb, c_prev_b)
            den = jnp.sum(qs, axis=1, keepdims=True) + sqn[:, ln:ln + 1]
            den = jnp.maximum(jnp.abs(den), enm[:, ln:ln + 1])
            y_ref[:, hs] = (so_ref[:, hs] * (num / den)).astype(BF16)
            upd.append(_dot_tn(kwb[:, ps], v_h))
        lo = GATE_I + 2 * pr
        so = jnp.where(srow < MLSTM_QKDIM, s_old[:, lo:lo + 1], s_old[:, lo + 1:lo + 2])
        sn = jnp.where(srow < MLSTM_QKDIM, s_new[:, lo:lo + 1], s_new[:, lo + 1:lo + 2])
        cst[pr] = so * c_prev + sn * jnp.where(srow < MLSTM_QKDIM, upd[0], upd[1])


def _mlstm(qk, v, so, gates, bi, bf, sel, *, nseq, interpret=False):
    b, l, _ = qk.shape
    full = lambda a: pl.BlockSpec(a.shape, lambda i, j, _n=a.ndim: (0,) * _n)
    tok = lambda width: pl.BlockSpec((nseq, CHUNK, width), lambda i, j: (i, j, 0))
    return pl.pallas_call(
        _mlstm_kernel,
        grid=(b // nseq, l // CHUNK),
        in_specs=[tok(2 * MLSTM_QK), tok(MLSTM_WIDTH), tok(MLSTM_WIDTH), tok(LANES),
                  full(bi), full(bf), full(sel)],
        out_specs=tok(MLSTM_WIDTH),
        out_shape=jax.ShapeDtypeStruct((b, l, MLSTM_WIDTH), BF16),
        scratch_shapes=[pltpu.VMEM((nseq, MLSTM_HEADS // 2, LANES, MLSTM_VDIM), F32),
                        pltpu.VMEM((nseq, MLSTM_QK, LANES), F32),
                        pltpu.VMEM((nseq, 1, LANES), F32)],
        compiler_params=pltpu.CompilerParams(
            dimension_semantics=("parallel", "arbitrary"), vmem_limit_bytes=VMEM_LIMIT),
        name="mlstm",
        interpret=interpret,
    )(qk, v, so, gates, bi, bf, sel)


def _outffn_kernel(x_ref, ya_ref, yb_ref, yc_ref, wo_ref, g1_ref, g2_ref, wg_ref, wu_ref,
                   wd_ref, g3_ref, o_ref):
    a0 = S5_WIDTH
    a1 = S5_WIDTH + SSD_WIDTH
    mix = (_dot(ya_ref[0], wo_ref[0:a0, :]) + _dot(yb_ref[0], wo_ref[a0:a1, :])
           + _dot(yc_ref[0], wo_ref[a1:, :]))
    x1 = x_ref[0] + _rms(mix, g1_ref[...])
    h = _rms(x1, g2_ref[...]).astype(BF16)
    act = (_silu(_dot(h, wg_ref[...])) * _dot(h, wu_ref[...])).astype(BF16)
    ff = _dot(act, wd_ref[...])
    o_ref[0] = x1 + _rms(ff, g3_ref[...])


def _out_ffn(x, ya, yb, yc, wo, g1, g2, wg, wu, wd, g3, *, tt, in_place, interpret=False):
    b, l, d = x.shape
    tok = lambda width: pl.BlockSpec((1, tt, width), lambda i, j: (i, j, 0))
    const = lambda a: pl.BlockSpec(a.shape, lambda i, j, _n=a.ndim: (0,) * _n,
                                   pipeline_mode=pl.Buffered(1))
    return pl.pallas_call(
        _outffn_kernel,
        grid=(b, l // tt),
        in_specs=[tok(d), tok(S5_WIDTH), tok(SSD_WIDTH), tok(MLSTM_WIDTH), const(wo), const(g1),
                  const(g2), const(wg), const(wu), const(wd), const(g3)],
        out_specs=tok(d),
        out_shape=jax.ShapeDtypeStruct((b, l, d), F32),
        input_output_aliases={0: 0} if in_place else {},
        compiler_params=pltpu.CompilerParams(
            dimension_semantics=("parallel", "parallel"), vmem_limit_bytes=VMEM_LIMIT),
        name="out_ffn",
        interpret=interpret,
    )(x, ya, yb, yc, wo, g1, g2, wg, wu, wd, g3)


def _lane_row(vals, offset):
    return jnp.zeros((1, LANES), F32).at[0, offset:offset + vals.shape[0]].set(vals)


def _pack_w_in(w_in):
    offs = [0]
    for s in IN_SPLIT_SIZES:
        offs.append(offs[-1] + s)
    seg = lambda i: w_in[..., offs[i]:offs[i + 1]]
    u, z, xbc, dt, qk, v, gi, gf, o = (seg(i) for i in range(9))
    pad = jnp.zeros(w_in.shape[:-1] + (LANES - SSD_HEADS - 2 * MLSTM_HEADS,), w_in.dtype)
    return jnp.concatenate([u, z, xbc, qk, v, o, dt, gi, gf, pad], axis=-1).astype(BF16)


def _layer(x, p, consts, *, in_place, tt_in, tt_out, tc, nseq, interpret):
    u, zx, qk, v, so, gates = _in_proj(x, p["norm_pre_mix"], p["w_in"], p["conv_w"], p["conv_b"],
                                       p["conv_scale"], tt=tt_in, interpret=interpret)
    ya = _s5(u, p["s5_bblk"], p["s5_cblk"], p["s5_are"], p["s5_aim"],
             p["s5_d"], p["s5_w_glu"], p["s5_b_glu"], tc=tc, interpret=interpret)
    yb = _ssd(zx, gates, p["ssd_dtb"], p["ssd_alog"], p["ssd_dsk"], p["ssd_norm"],
              consts["ssd_sel"], nseq=nseq, interpret=interpret)
    yc = _mlstm(qk, v, so, gates, p["mlstm_bi"], p["mlstm_bf"], consts["mlstm_sel"], nseq=nseq,
                interpret=interpret)
    return _out_ffn(x, ya, yb, yc, p["w_out"], p["norm_post_mix"],
                    p["norm_pre_ffn"], p["w_gate"], p["w_up"], p["w_down"], p["norm_post_ffn"],
                    tt=tt_out, in_place=in_place, interpret=interpret)


def _prepare_params(norm_pre_mix, w_in, s5_lambda_re, s5_lambda_im, s5_log_step, s5_b_re, s5_b_im,
                    s5_c_re, s5_c_im, s5_d, s5_w_glu, s5_b_glu, ssd_conv_w, ssd_conv_b,
                    ssd_dt_bias, ssd_a_log, ssd_d, ssd_norm, mlstm_conv_w, mlstm_conv_b, mlstm_b_i,
                    mlstm_b_f, w_out, norm_post_mix, norm_pre_ffn, w_gate, w_up, w_down,
                    norm_post_ffn):
    depth = w_in.shape[0]
    row = lambda a: a.reshape(depth, 1, -1).astype(F32)
    bblk, cblk, a_re, a_im = jax.vmap(_s5_prepare)(
        s5_lambda_re.astype(F32), s5_lambda_im.astype(F32), s5_log_step.astype(F32),
        s5_b_re.astype(F32), s5_b_im.astype(F32), s5_c_re.astype(F32), s5_c_im.astype(F32))
    lanes = lambda a, off: jax.vmap(lambda v: _lane_row(v.astype(F32), off))(a)
    conv_scale = jnp.concatenate(
        [jnp.ones((SSD_XBC + MLSTM_QK,), F32), jnp.full((MLSTM_QK,), MLSTM_QKDIM ** -0.5, F32)])
    return dict(
        norm_pre_mix=row(norm_pre_mix), w_in=_pack_w_in(w_in),
        conv_w=jnp.concatenate([ssd_conv_w, mlstm_conv_w], axis=-1).astype(F32),
        conv_b=row(jnp.concatenate([ssd_conv_b, mlstm_conv_b], axis=-1)),
        conv_scale=jnp.broadcast_to(conv_scale, (depth, 1, CONV_W)),
        s5_bblk=bblk, s5_cblk=cblk, s5_are=a_re, s5_aim=a_im, s5_d=row(s5_d),
        s5_w_glu=s5_w_glu.astype(BF16), s5_b_glu=row(s5_b_glu),
        ssd_dtb=lanes(ssd_dt_bias, GATE_DT), ssd_alog=lanes(ssd_a_log, GATE_DT),
        ssd_dsk=row(jnp.repeat(ssd_d, SSD_HEADDIM, axis=-1)), ssd_norm=row(ssd_norm),
        mlstm_bi=lanes(mlstm_b_i, GATE_I), mlstm_bf=lanes(mlstm_b_f, GATE_F),
        w_out=w_out.astype(BF16), norm_post_mix=row(norm_post_mix),
        norm_pre_ffn=row(norm_pre_ffn), w_gate=w_gate.astype(BF16), w_up=w_up.astype(BF16),
        w_down=w_down.astype(BF16), norm_post_ffn=row(norm_post_ffn))


def _forward(x, *params, tt_in=256, tt_out=512, tc=32, nseq=4, interpret=False):
    stacked = _prepare_params(*params)
    consts = dict(ssd_sel=_expand_matrix(GATE_DT, SSD_HEADS, SSD_HEADDIM),
                  mlstm_sel=_expand_matrix(GATE_I, MLSTM_HEADS, MLSTM_QKDIM))
    for layer in range(stacked["w_in"].shape[0]):
        p = {k: v[layer] for k, v in stacked.items()}
        x = _layer(x, p, consts, in_place=layer > 0, tt_in=tt_in, tt_out=tt_out, tc=tc,
                   nseq=nseq, interpret=interpret)
    return x


def kernel(x, norm_pre_mix, w_in, s5_lambda_re, s5_lambda_im, s5_log_step, s5_b_re, s5_b_im, s5_c_re, s5_c_im, s5_d, s5_w_glu, s5_b_glu, ssd_conv_w, ssd_conv_b, ssd_dt_bias, ssd_a_log, ssd_d, ssd_norm, mlstm_conv_w, mlstm_conv_b, mlstm_b_i, mlstm_b_f, w_out, norm_post_mix, norm_pre_ffn, w_gate, w_up, w_down, norm_post_ffn):
    return _forward(x, norm_pre_mix, w_in, s5_lambda_re, s5_lambda_im, s5_log_step, s5_b_re,
                    s5_b_im, s5_c_re, s5_c_im, s5_d, s5_w_glu, s5_b_glu, ssd_conv_w, ssd_conv_b,
                    ssd_dt_bias, ssd_a_log, ssd_d, ssd_norm, mlstm_conv_w, mlstm_conv_b, mlstm_b_i,
                    mlstm_b_f, w_out, norm_post_mix, norm_pre_ffn, w_gate, w_up, w_down,
                    norm_post_ffn)
```

```python
import functools
import math

import jax
import jax.numpy as jnp
from jax import lax
from jax.experimental import pallas as pl
from jax.experimental.pallas import tpu as pltpu

F32 = jnp.float32
BF16 = jnp.bfloat16

D_MODEL = 1024
D_MIX = 2048
S5_WIDTH = 512
S5_GROUP = 16
S5_GROUPS = 32
S5_STATE = 64
S5_NSTATE = S5_GROUPS * S5_STATE
SSD_HEADDIM = 64
SSD_WIDTH = 768
SSD_HEADS = 12
SSD_GROUPS = 2
SSD_HPG = SSD_HEADS // SSD_GROUPS
SSD_GW = SSD_WIDTH // SSD_GROUPS
SSD_STATE = 128
SSD_XBC = SSD_WIDTH + 2 * SSD_GROUPS * SSD_STATE
MLSTM_WIDTH = 768
MLSTM_HEADS = 6
MLSTM_VDIM = 128
MLSTM_QKDIM = 64
MLSTM_QK = 384
CONV_K = 4
D_FF = 2816
NORM_EPS = 1e-6
IN_SPLIT_SIZES = (S5_WIDTH, SSD_WIDTH, SSD_XBC, SSD_HEADS, 2 * MLSTM_QK, MLSTM_WIDTH,
                  MLSTM_HEADS, MLSTM_HEADS, MLSTM_WIDTH)

LANES = 128
CHUNK = 128
HALO = 8
LOG2E = math.log2(math.e)
GATE_DT = 0
GATE_I = SSD_HEADS
GATE_F = SSD_HEADS + MLSTM_HEADS
P_U = 0
P_Z = P_U + S5_WIDTH
P_CONV = P_Z + SSD_WIDTH
P_V = P_CONV + SSD_XBC + 2 * MLSTM_QK
P_O = P_V + MLSTM_WIDTH
P_G = P_O + MLSTM_WIDTH
D_INP = P_G + LANES
CONV_W = SSD_XBC + 2 * MLSTM_QK

VMEM_LIMIT = 56 * 1024 * 1024


def _silu(x):
    hx = 0.5 * x
    return hx + hx * jnp.tanh(hx)


def _sigmoid(x):
    return 0.5 + 0.5 * jnp.tanh(0.5 * x)


def _softplus(x):
    return jnp.maximum(x, 0.0) + jnp.log1p(jnp.exp(-jnp.abs(x)))


def _rms(x, g):
    return x * lax.rsqrt(jnp.mean(x * x, axis=-1, keepdims=True) + NORM_EPS) * g


def _dot(a, b):
    return jnp.dot(a, b, preferred_element_type=F32)


def _dot_nt(a, b):
    return lax.dot_general(a, b, (((1,), (1,)), ((), ())), preferred_element_type=F32)


def _dot_tn(a, b):
    return lax.dot_general(a, b, (((0,), (0,)), ((), ())), preferred_element_type=F32)


def _cumsum_rows(x):
    t = x.shape[0]
    r = lax.broadcasted_iota(jnp.int32, (t, t), 0)
    c = lax.broadcasted_iota(jnp.int32, (t, t), 1)
    tril = jnp.where(c <= r, 1.0, 0.0).astype(F32)
    return jnp.dot(tril, x, precision=lax.Precision.HIGHEST, preferred_element_type=F32)


def _cummax_rows_blocked(x, blk):
    pos = lax.broadcasted_iota(jnp.int32, x.shape, 0) % blk
    d = 1
    while d < blk:
        x = jnp.maximum(x, jnp.where(pos >= d, pltpu.roll(x, d, axis=0), -jnp.inf))
        d *= 2
    return x


def _split3(x):
    hi = x.astype(BF16)
    r1 = x - hi.astype(F32)
    mid = r1.astype(BF16)
    lo = (r1 - mid.astype(F32)).astype(BF16)
    return hi, mid, lo


def _expand_matrix(first, heads, width):
    src = jnp.arange(LANES)[:, None] - first
    dst = jnp.arange(heads * width)[None, :]
    sel = ((dst >= src * width) & (dst < (src + 1) * width)).astype(BF16)
    return jnp.concatenate([sel, sel, sel], axis=0)


def _expand_heads(v, sel3_ref):
    return _dot(jnp.concatenate(_split3(v), axis=1), sel3_ref[...])


def _half_masks(t, width):
    lane = lax.broadcasted_iota(jnp.int32, (t, LANES), 1)
    lo = jnp.where(lane < width, 1.0, 0.0).astype(BF16)
    hi = jnp.where(lane >= width, 1.0, 0.0).astype(BF16)
    return lo, hi


CONV_CW = 256


def _shift_rows(x3, prev3, j):
    rolled = pltpu.roll(jnp.concatenate([prev3, x3], axis=0), j, axis=1)
    sub = lax.broadcasted_iota(jnp.int32, x3.shape, 1)
    return jnp.where(sub >= j, rolled[1:], rolled[:-1])


def _causal_conv4(x3, w_ref, cs):
    zero = jnp.zeros((1,) + x3.shape[1:], F32)
    x1 = _shift_rows(x3, zero, 1)
    a = w_ref[3:4, cs] * x3[1:] + w_ref[2:3, cs] * x1[1:]
    b = w_ref[1:2, cs] * x3 + w_ref[0:1, cs] * x1
    return a + _shift_rows(b[1:], b[0:1], 2)


def _inproj_kernel(x_ref, g_ref, w_ref, cw_ref, cb_ref, cs_ref,
                   u_ref, zx_ref, qk_ref, v_ref, so_ref, gt_ref, carry):
    t = x_ref.shape[1]

    @pl.when(pl.program_id(1) == 0)
    def _():
        carry[...] = jnp.zeros_like(carry)

    h = _rms(x_ref[0], g_ref[...]).astype(BF16)
    u_ref[0] = _dot(h, w_ref[:, P_U:P_Z])
    zx_ref[0, :, 0:SSD_WIDTH] = _silu(_dot(h, w_ref[:, P_Z:P_CONV]))
    for c in range(CONV_W // CONV_CW):
        cs = slice(c * CONV_CW, (c + 1) * CONV_CW)
        pc = _dot(h, w_ref[:, P_CONV + c * CONV_CW:P_CONV + (c + 1) * CONV_CW])
        x3 = jnp.concatenate([carry[:, cs], pc], axis=0).reshape(t // HALO + 1, HALO, CONV_CW)
        carry[:, cs] = pc[t - HALO:, :]
        acc = _causal_conv4(x3, cw_ref, cs).reshape(t, CONV_CW) + cb_ref[:, cs]
        act = _silu(acc)
        if (c + 1) * CONV_CW > SSD_XBC + MLSTM_QK:
            act = act * cs_ref[:, cs]
        if (c + 1) * CONV_CW <= SSD_XBC:
            zx_ref[0, :, SSD_WIDTH + c * CONV_CW:SSD_WIDTH + (c + 1) * CONV_CW] = act
        else:
            qk_ref[0, :, c * CONV_CW - SSD_XBC:(c + 1) * CONV_CW - SSD_XBC] = act
    v_ref[0] = _dot(h, w_ref[:, P_V:P_O]).astype(BF16)
    so_ref[0] = _sigmoid(_dot(h, w_ref[:, P_O:P_G]))
    gt_ref[0] = _dot(h, w_ref[:, P_G:D_INP])


def _in_proj(x, g, w, conv_w, conv_b, conv_scale, *, tt, interpret=False):
    b, l, d = x.shape
    tok = lambda width: pl.BlockSpec((1, tt, width), lambda i, j: (i, j, 0))
    const = lambda a: pl.BlockSpec(a.shape, lambda i, j, _n=a.ndim: (0,) * _n)
    sds = lambda width, dt: jax.ShapeDtypeStruct((b, l, width), dt)
    return pl.pallas_call(
        _inproj_kernel,
        grid=(b, l // tt),
        in_specs=[tok(d), const(g), const(w), const(conv_w), const(conv_b), const(conv_scale)],
        out_specs=[tok(S5_WIDTH), tok(SSD_WIDTH + SSD_XBC), tok(2 * MLSTM_QK), tok(MLSTM_WIDTH),
                   tok(MLSTM_WIDTH), tok(LANES)],
        out_shape=[sds(S5_WIDTH, F32), sds(SSD_WIDTH + SSD_XBC, F32), sds(2 * MLSTM_QK, F32),
                   sds(MLSTM_WIDTH, BF16), sds(MLSTM_WIDTH, F32), sds(LANES, F32)],
        scratch_shapes=[pltpu.VMEM((HALO, CONV_W), F32)],
        compiler_params=pltpu.CompilerParams(
            dimension_semantics=("parallel", "arbitrary"), vmem_limit_bytes=VMEM_LIMIT),
        name="in_proj",
        interpret=interpret,
    )(x, g, w, conv_w, conv_b, conv_scale)


S5_KB = 2
S5_KW = S5_WIDTH // S5_KB
S5_HALF = S5_NSTATE // S5_KB
S5_STRIP = 256


def _s5_kernel(u_ref, bblk_ref, cblk_ref, are_ref, aim_ref, d_ref, wglu_ref, bglu_ref,
               y_ref, bu_ref, xs_ref, st_ref, *, nb, tc):
    @pl.when(pl.program_id(0) == 0)
    def _():
        st_ref[...] = jnp.zeros_like(st_ref)

    m = nb * tc
    u = pltpu.einshape("btd->tbd", u_ref[...]).reshape(m, S5_WIDTH)
    ub = u.astype(BF16)
    ys = []
    for kb in range(S5_KB):
        ukb = ub[:, kb * S5_KW:(kb + 1) * S5_KW]
        ykb = None
        for s in range(S5_HALF // S5_STRIP):
            c_re = kb * 2 * S5_HALF + s * S5_STRIP
            c_im = c_re + S5_HALF
            c_a = kb * S5_HALF + s * S5_STRIP
            re_cols = slice(c_re, c_re + S5_STRIP)
            im_cols = slice(c_im, c_im + S5_STRIP)
            w_re = slice(s * S5_STRIP, (s + 1) * S5_STRIP)
            w_im = slice(S5_HALF + s * S5_STRIP, S5_HALF + (s + 1) * S5_STRIP)
            bu_ref[:, re_cols] = _dot(ukb, bblk_ref[kb, :, w_re])
            bu_ref[:, im_cols] = _dot(ukb, bblk_ref[kb, :, w_im])
            ar = jnp.broadcast_to(are_ref[:, c_a:c_a + S5_STRIP], (nb, S5_STRIP))
            ai = jnp.broadcast_to(aim_ref[:, c_a:c_a + S5_STRIP], (nb, S5_STRIP))
            xr = st_ref[:, re_cols]
            xi = st_ref[:, im_cols]
            for t in range(tc):
                rows = slice(t * nb, (t + 1) * nb)
                xr, xi = (ar * xr - ai * xi + bu_ref[rows, re_cols],
                          ar * xi + ai * xr + bu_ref[rows, im_cols])
                xs_ref[rows, re_cols] = xr.astype(BF16)
                xs_ref[rows, im_cols] = xi.astype(BF16)
            st_ref[:, re_cols] = xr
            st_ref[:, im_cols] = xi
            part = (_dot(xs_ref[:, re_cols], cblk_ref[kb, w_re, :])
                    + _dot(xs_ref[:, im_cols], cblk_ref[kb, w_im, :]))
            ykb = part if ykb is None else ykb + part
        ys.append(ykb)
    y = jnp.concatenate(ys, axis=1) + d_ref[...] * u
    g = jax.nn.gelu(y)
    out = g * _sigmoid(_dot(g.astype(BF16), wglu_ref[...]) + bglu_ref[...])
    y_ref[...] = pltpu.einshape("tbd->btd", out.reshape(tc, nb, S5_WIDTH)).astype(BF16)


def _s5(u, bblk, cblk, a_re, a_im, d_row, w_glu, b_glu, *, tc, interpret=False):
    nb, l, _ = u.shape
    m = nb * tc
    full = lambda a: pl.BlockSpec(a.shape, lambda i, _n=a.ndim: (0,) * _n)
    blk = pl.BlockSpec((nb, tc, S5_WIDTH), lambda i: (0, i, 0))
    return pl.pallas_call(
        functools.partial(_s5_kernel, nb=nb, tc=tc),
        grid=(l // tc,),
        in_specs=[blk, full(bblk), full(cblk), full(a_re), full(a_im), full(d_row),
                  full(w_glu), full(b_glu)],
        out_specs=blk,
        out_shape=jax.ShapeDtypeStruct((nb, l, S5_WIDTH), BF16),
        scratch_shapes=[pltpu.VMEM((m, 2 * S5_NSTATE), F32),
                        pltpu.VMEM((m, 2 * S5_NSTATE), BF16),
                        pltpu.VMEM((nb, 2 * S5_NSTATE), F32)],
        compiler_params=pltpu.CompilerParams(
            dimension_semantics=("arbitrary",), vmem_limit_bytes=VMEM_LIMIT),
        name="s5",
        interpret=interpret,
    )(u, bblk, cblk, a_re, a_im, d_row, w_glu, b_glu)


def _s5_prepare(lam_re, lam_im, log_step, b_re, b_im, c_re, c_im):
    step = jnp.exp(log_step)[:, None]
    mag = jnp.exp(lam_re * step)
    ang = lam_im * step
    a_re = mag * jnp.cos(ang)
    a_im = mag * jnp.sin(ang)
    den = lam_re * lam_re + lam_im * lam_im
    nr = a_re - 1.0
    f_re = (nr * lam_re + a_im * lam_im) / den
    f_im = (a_im * lam_re - nr * lam_im) / den
    bb_re = f_re[..., None] * b_re - f_im[..., None] * b_im
    bb_im = f_re[..., None] * b_im + f_im[..., None] * b_re
    gl = S5_GROUPS // S5_KB
    eye = jnp.eye(gl, dtype=F32)

    def b_block(bb):
        bb = bb.reshape(S5_KB, gl, S5_STATE, S5_GROUP)
        blk = jnp.einsum('kgph,gq->kghqp', bb, eye)
        return blk.reshape(S5_KB, gl * S5_GROUP, gl * S5_STATE)

    def c_block(cc):
        cc = cc.reshape(S5_KB, gl, S5_GROUP, S5_STATE)
        blk = jnp.einsum('kghp,gq->kgpqh', cc, eye)
        return blk.reshape(S5_KB, gl * S5_STATE, gl * S5_GROUP)

    bblk = jnp.concatenate([b_block(bb_re), b_block(bb_im)], axis=2).astype(BF16)
    cblk = jnp.concatenate([c_block(c_re), -c_block(c_im)], axis=1).astype(BF16)
    return bblk, cblk, a_re.reshape(1, S5_NSTATE), a_im.reshape(1, S5_NSTATE)


def _ssd_kernel(zx_ref, gt_ref, dtb_ref, alog_ref, dsk_ref, ng_ref, sel_ref, y_ref, hst):
    ns = zx_ref.shape[0]
    t = CHUNK
    rows = ns * t

    @pl.when(pl.program_id(1) == 0)
    def _():
        hst[...] = jnp.zeros_like(hst)

    seq = lambda a, sq: a[sq * t:(sq + 1) * t]
    stack = lambda parts: jnp.concatenate(parts, axis=0)
    per_seq = lambda vals: stack([jnp.broadcast_to(v, (t, v.shape[1])) for v in vals])
    col_blk = lambda a, lo, n: a[:, :, lo:lo + n].reshape(rows, n)

    xs = col_blk(zx_ref, SSD_WIDTH, SSD_WIDTH)
    dt = _softplus(gt_ref[...].reshape(rows, LANES) + dtb_ref[...])
    dta = dt * (-jnp.exp(alog_ref[...]))
    acs = stack([_cumsum_rows(seq(dta, sq)) for sq in range(ns)]) * LOG2E
    acs_t = [seq(acs, sq).T for sq in range(ns)]
    dt_e = _expand_heads(dt, sel_ref)
    acs_e = _expand_heads(acs, sel_ref)
    last = [acs_e[(sq + 1) * t - 1:(sq + 1) * t, :] for sq in range(ns)]
    xdt = xs * dt_e
    xdt_b = xdt.astype(BF16)
    xw_b = (xdt * jnp.exp2(per_seq(last) - acs_e)).astype(BF16)
    eacs_e = jnp.exp2(acs_e)
    cdec = [jnp.exp2(v) for v in last]

    pos = lax.broadcasted_iota(jnp.int32, (rows, t), 0) % t
    causal = lax.broadcasted_iota(jnp.int32, (rows, t), 1) <= pos
    masks = _half_masks(rows, SSD_HEADDIM)
    ys = []
    for g in range(SSD_GROUPS):
        b0 = 2 * SSD_WIDTH + g * SSD_STATE
        c0 = 2 * SSD_WIDTH + SSD_GROUPS * SSD_STATE + g * SSD_STATE
        bm = col_blk(zx_ref, b0, SSD_STATE).astype(BF16)
        cm = col_blk(zx_ref, c0, SSD_STATE).astype(BF16)
        cbm = stack([_dot_nt(seq(cm, sq), seq(bm, sq)) for sq in range(ns)])
        gs = slice(g * SSD_GW, (g + 1) * SSD_GW)
        h_prev = [hst[sq, g] for sq in range(ns)]
        y_g = stack([_dot(seq(cm, sq), h_prev[sq].astype(BF16)) for sq in range(ns)]) * eacs_e[:, gs]
        diag = []
        for pr in range(SSD_HPG // 2):
            ps = slice(g * SSD_GW + pr * LANES, g * SSD_GW + (pr + 1) * LANES)
            acc = None
            for q in range(2):
                j = g * SSD_HPG + pr * 2 + q
                a_row = stack([jnp.broadcast_to(acs_t[sq][j:j + 1, :], (t, t)) for sq in range(ns)])
                seg = acs[:, j:j + 1] - a_row
                m = (cbm * jnp.exp2(jnp.where(causal, seg, -jnp.inf))).astype(BF16)
                xm = xdt_b[:, ps] * masks[q]
                part = stack([_dot(seq(m, sq), seq(xm, sq)) for sq in range(ns)])
                acc = part if acc is None else acc + part
            diag.append(acc)
        ys.append(y_g + jnp.concatenate(diag, axis=1))
        for sq in range(ns):
            hst[sq, g] = h_prev[sq] * cdec[sq][:, gs] + _dot_tn(seq(bm, sq), seq(xw_b, sq)[:, gs])
    y = jnp.concatenate(ys, axis=1) + xs * dsk_ref[...]
    y = y * col_blk(zx_ref, 0, SSD_WIDTH)
    outs = []
    for g in range(SSD_GROUPS):
        gs = slice(g * SSD_GW, (g + 1) * SSD_GW)
        outs.append(_rms(y[:, gs], ng_ref[:, gs]))
    y_ref[...] = jnp.concatenate(outs, axis=1).astype(BF16).reshape(ns, t, SSD_WIDTH)


def _ssd(zx, gates, dtb, alog, dsk, ng, sel, *, nseq, interpret=False):
    b, l, _ = zx.shape
    full = lambda a: pl.BlockSpec(a.shape, lambda i, j, _n=a.ndim: (0,) * _n)
    tok = lambda width: pl.BlockSpec((nseq, CHUNK, width), lambda i, j: (i, j, 0))
    return pl.pallas_call(
        _ssd_kernel,
        grid=(b // nseq, l // CHUNK),
        in_specs=[tok(SSD_WIDTH + SSD_XBC), tok(LANES), full(dtb), full(alog), full(dsk), full(ng),
                  full(sel)],
        out_specs=tok(SSD_WIDTH),
        out_shape=jax.ShapeDtypeStruct((b, l, SSD_WIDTH), BF16),
        scratch_shapes=[pltpu.VMEM((nseq, SSD_GROUPS, SSD_STATE, SSD_GW), F32)],
        compiler_params=pltpu.CompilerParams(
            dimension_semantics=("parallel", "arbitrary"), vmem_limit_bytes=VMEM_LIMIT),
        name="ssd",
        interpret=interpret,
    )(zx, gates, dtb, alog, dsk, ng, sel)


def _mlstm_kernel(qk_ref, v_ref, so_ref, gt_ref, bi_ref, bf_ref, sel_ref, y_ref, cst, nst, mst):
    ns = qk_ref.shape[0]
    t = CHUNK
    rows = ns * t

    @pl.when(pl.program_id(1) == 0)
    def _():
        cst[...] = jnp.zeros_like(cst)
        nst[...] = jnp.zeros_like(nst)
        mst[...] = jnp.zeros_like(mst)

    seq = lambda a, sq: a[sq * t:(sq + 1) * t]
    last = lambda a, sq: a[(sq + 1) * t - 1:(sq + 1) * t]
    stack = lambda parts: jnp.concatenate(parts, axis=0)
    per_seq = lambda vals: stack([jnp.broadcast_to(v, (t, v.shape[1])) for v in vals])

    gates = gt_ref[...].reshape(rows, LANES)
    log_i = gates + bi_ref[...]
    log_f = -_softplus(-(gates + bf_ref[...]))
    log_f = pltpu.roll(log_f, LANES - (GATE_F - GATE_I), axis=1)
    bcum = stack([_cumsum_rows(seq(log_f, sq)) for sq in range(ns)])
    gvec = log_i - bcum
    cmax = _cummax_rows_blocked(gvec, t)
    g_tot = [last(bcum, sq) for sq in range(ns)]
    gmax = [last(cmax, sq) for sq in range(ns)]
    m_prev = [mst[sq] for sq in range(ns)]
    w_inter = bcum + per_seq(m_prev)
    m_row = jnp.maximum(w_inter, bcum + cmax)
    alpha2 = (bcum - m_row) * LOG2E
    s_int = jnp.exp(w_inter - m_row)
    enm = jnp.exp(-m_row)
    e_end = jnp.exp(gvec - per_seq(gmax))
    gvec2_t = [seq(gvec, sq).T * LOG2E for sq in range(ns)]
    s_old, s_new = [], []
    for sq in range(ns):
        a_loc = g_tot[sq] + gmax[sq]
        m_new = jnp.maximum(g_tot[sq] + m_prev[sq], a_loc)
        s_old.append(jnp.exp(g_tot[sq] + m_prev[sq] - m_new))
        s_new.append(jnp.exp(a_loc - m_new))
        mst[sq] = m_new

    q = qk_ref[:, :, 0:MLSTM_QK].reshape(rows, MLSTM_QK)
    k = qk_ref[:, :, MLSTM_QK:].reshape(rows, MLSTM_QK)
    qb = q.astype(BF16)
    kb = k.astype(BF16)
    e_end_b = e_end.astype(BF16)
    nrow = lax.broadcasted_iota(jnp.int32, (MLSTM_QK, LANES), 0)
    nlane = lax.broadcasted_iota(jnp.int32, (MLSTM_QK, LANES), 1)
    own = (nlane >= GATE_I) & ((nlane - GATE_I) * MLSTM_QKDIM <= nrow) & (
        nrow < (nlane - GATE_I + 1) * MLSTM_QKDIM)
    qn = []
    for sq in range(ns):
        n_prev = nst[sq]
        qn.append(_dot(seq(qb, sq), n_prev.astype(BF16)))
        n_loc = jnp.where(own, _dot_tn(seq(kb, sq), seq(e_end_b, sq)), 0.0)
        nst[sq] = s_old[sq] * n_prev + s_new[sq] * n_loc
    sqn = s_int * stack(qn)
    kwb = (k * _expand_heads(e_end, sel_ref)).astype(BF16)

    pos = lax.broadcasted_iota(jnp.int32, (rows, t), 0) % t
    causal = lax.broadcasted_iota(jnp.int32, (rows, t), 1) <= pos
    masks = _half_masks(rows, MLSTM_QKDIM)
    srow = lax.broadcasted_iota(jnp.int32, (LANES, MLSTM_VDIM), 0)
    for pr in range(MLSTM_HEADS // 2):
        ps = slice(pr * LANES, (pr + 1) * LANES)
        c_prev = [cst[sq, pr] for sq in range(ns)]
        c_prev_b = [c.astype(BF16) for c in c_prev]
        upd = [[None, None] for _ in range(ns)]
        for hq in range(2):
            h = pr * 2 + hq
            ln = GATE_I + h
            hs = slice(h * MLSTM_VDIM, (h + 1) * MLSTM_VDIM)
            q_hb = qb[:, ps] * masks[hq]
            s = stack([_dot_nt(seq(q_hb, sq), seq(kb, sq)[:, ps]) for sq in range(ns)])
            g_row = stack([jnp.broadcast_to(gvec2_t[sq][ln:ln + 1, :], (t, t)) for sq in range(ns)])
            dm = jnp.where(causal, alpha2[:, ln:ln + 1] + g_row, -jnp.inf)
            qs = s * jnp.exp2(dm)
            qsb = qs.astype(BF16)
            num = (stack([_dot(seq(qsb, sq), v_ref[sq, :, hs]) for sq in range(ns)])
                   + s_int[:, ln:ln + 1]
                   * stack([_dot(seq(q_hb, sq), c_prev_b[sq]) for sq in range(ns)]))
            den = jnp.sum(qs, axis=1, keepdims=True) + sqn[:, ln:ln + 1]
            den = jnp.maximum(jnp.abs(den), enm[:, ln:ln + 1])
            out = so_ref[:, :, hs].reshape(rows, MLSTM_VDIM) * (num / den)
            y_ref[:, :, hs] = out.astype(BF16).reshape(ns, t, MLSTM_VDIM)
            for sq in range(ns):
                upd[sq][hq] = _dot_tn(seq(kwb, sq)[:, ps], v_ref[sq, :, hs])
        lo = GATE_I + 2 * pr
        for sq in range(ns):
            so = jnp.where(srow < MLSTM_QKDIM, s_old[sq][:, lo:lo + 1], s_old[sq][:, lo + 1:lo + 2])
            sn = jnp.where(srow < MLSTM_QKDIM, s_new[sq][:, lo:lo + 1], s_new[sq][:, lo + 1:lo + 2])
            cst[sq, pr] = so * c_prev[sq] + sn * jnp.where(srow < MLSTM_QKDIM, upd[sq][0], upd[sq][1])


def _mlstm(qk, v, so, gates, bi, bf, sel, *, nseq, interpret=False):
    b, l, _ = qk.shape
    full = lambda a: pl.BlockSpec(a.shape, lambda i, j, _n=a.ndim: (0,) * _n)
    tok = lambda width: pl.BlockSpec((nseq, CHUNK, width), lambda i, j: (i, j, 0))
    return pl.pallas_call(
        _mlstm_kernel,
        grid=(b // nseq, l // CHUNK),
        in_specs=[tok(2 * MLSTM_QK), tok(MLSTM_WIDTH), tok(MLSTM_WIDTH), tok(LANES),
                  full(bi), full(bf), full(sel)],
        out_specs=tok(MLSTM_WIDTH),
        out_shape=jax.ShapeDtypeStruct((b, l, MLSTM_WIDTH), BF16),
        scratch_shapes=[pltpu.VMEM((nseq, MLSTM_HEADS // 2, LANES, MLSTM_VDIM), F32),
                        pltpu.VMEM((nseq, MLSTM_QK, LANES), F32),
                        pltpu.VMEM((nseq, 1, LANES), F32)],
        compiler_params=pltpu.CompilerParams(
            dimension_semantics=("parallel", "arbitrary"), vmem_limit_bytes=VMEM_LIMIT),
        name="mlstm",
        interpret=interpret,
    )(qk, v, so, gates, bi, bf, sel)


def _outffn_kernel(x_ref, ya_ref, yb_ref, yc_ref, wo_ref, g1_ref, g2_ref, wg_ref, wu_ref,
                   wd_ref, g3_ref, o_ref):
    a0 = S5_WIDTH
    a1 = S5_WIDTH + SSD_WIDTH
    mix = (_dot(ya_ref[0], wo_ref[0:a0, :]) + _dot(yb_ref[0], wo_ref[a0:a1, :])
           + _dot(yc_ref[0], wo_ref[a1:, :]))
    x1 = x_ref[0] + _rms(mix, g1_ref[...])
    h = _rms(x1, g2_ref[...]).astype(BF16)
    act = (_silu(_dot(h, wg_ref[...])) * _dot(h, wu_ref[...])).astype(BF16)
    ff = _dot(act, wd_ref[...])
    o_ref[0] = x1 + _rms(ff, g3_ref[...])


def _out_ffn(x, ya, yb, yc, wo, g1, g2, wg, wu, wd, g3, *, tt, in_place, interpret=False):
    b, l, d = x.shape
    tok = lambda width: pl.BlockSpec((1, tt, width), lambda i, j: (i, j, 0))
    const = lambda a: pl.BlockSpec(a.shape, lambda i, j, _n=a.ndim: (0,) * _n,
                                   pipeline_mode=pl.Buffered(1))
    return pl.pallas_call(
        _outffn_kernel,
        grid=(b, l // tt),
        in_specs=[tok(d), tok(S5_WIDTH), tok(SSD_WIDTH), tok(MLSTM_WIDTH), const(wo), const(g1),
                  const(g2), const(wg), const(wu), const(wd), const(g3)],
        out_specs=tok(d),
        out_shape=jax.ShapeDtypeStruct((b, l, d), F32),
        input_output_aliases={0: 0} if in_place else {},
        compiler_params=pltpu.CompilerParams(
            dimension_semantics=("parallel", "parallel"), vmem_limit_bytes=VMEM_LIMIT),
        name="out_ffn",
        interpret=interpret,
    )(x, ya, yb, yc, wo, g1, g2, wg, wu, wd, g3)


def _lane_row(vals, offset):
    return jnp.zeros((1, LANES), F32).at[0, offset:offset + vals.shape[0]].set(vals)


def _pack_w_in(w_in):
    offs = [0]
    for s in IN_SPLIT_SIZES:
        offs.append(offs[-1] + s)
    seg = lambda i: w_in[..., offs[i]:offs[i + 1]]
    u, z, xbc, dt, qk, v, gi, gf, o = (seg(i) for i in range(9))
    pad = jnp.zeros(w_in.shape[:-1] + (LANES - SSD_HEADS - 2 * MLSTM_HEADS,), w_in.dtype)
    return jnp.concatenate([u, z, xbc, qk, v, o, dt, gi, gf, pad], axis=-1).astype(BF16)


def _layer(x, p, consts, *, in_place, tt_in, tt_out, tc, nseq_ssd, nseq_mlstm, interpret):
    u, zx, qk, v, so, gates = _in_proj(x, p["norm_pre_mix"], p["w_in"], p["conv_w"], p["conv_b"],
                                       p["conv_scale"], tt=tt_in, interpret=interpret)
    ya = _s5(u, p["s5_bblk"], p["s5_cblk"], p["s5_are"], p["s5_aim"],
             p["s5_d"], p["s5_w_glu"], p["s5_b_glu"], tc=tc, interpret=interpret)
    yb = _ssd(zx, gates, p["ssd_dtb"], p["ssd_alog"], p["ssd_dsk"], p["ssd_norm"],
              consts["ssd_sel"], nseq=nseq_ssd, interpret=interpret)
    yc = _mlstm(qk, v, so, gates, p["mlstm_bi"], p["mlstm_bf"], consts["mlstm_sel"],
                nseq=nseq_mlstm, interpret=interpret)
    return _out_ffn(x, ya, yb, yc, p["w_out"], p["norm_post_mix"],
                    p["norm_pre_ffn"], p["w_gate"], p["w_up"], p["w_down"], p["norm_post_ffn"],
                    tt=tt_out, in_place=in_place, interpret=interpret)


def _prepare_layer(norm_pre_mix, w_in, s5_lambda_re, s5_lambda_im, s5_log_step, s5_b_re, s5_b_im,
                   s5_c_re, s5_c_im, s5_d, s5_w_glu, s5_b_glu, ssd_conv_w, ssd_conv_b,
                   ssd_dt_bias, ssd_a_log, ssd_d, ssd_norm, mlstm_conv_w, mlstm_conv_b, mlstm_b_i,
                   mlstm_b_f, w_out, norm_post_mix, norm_pre_ffn, w_gate, w_up, w_down,
                   norm_post_ffn):
    f32 = lambda a: a.astype(F32)
    row = lambda a: f32(a).reshape(1, -1)
    bblk, cblk, a_re, a_im = _s5_prepare(
        f32(s5_lambda_re), f32(s5_lambda_im), f32(s5_log_step), f32(s5_b_re), f32(s5_b_im),
        f32(s5_c_re), f32(s5_c_im))
    conv_scale = jnp.concatenate(
        [jnp.ones((SSD_XBC + MLSTM_QK,), F32), jnp.full((MLSTM_QK,), MLSTM_QKDIM ** -0.5, F32)])
    return dict(
        norm_pre_mix=row(norm_pre_mix), w_in=_pack_w_in(w_in),
        conv_w=f32(jnp.concatenate([ssd_conv_w, mlstm_conv_w], axis=-1)),
        conv_b=row(jnp.concatenate([ssd_conv_b, mlstm_conv_b], axis=-1)),
        conv_scale=conv_scale.reshape(1, CONV_W),
        s5_bblk=bblk, s5_cblk=cblk, s5_are=a_re, s5_aim=a_im, s5_d=row(s5_d),
        s5_w_glu=s5_w_glu.astype(BF16), s5_b_glu=row(s5_b_glu),
        ssd_dtb=_lane_row(f32(ssd_dt_bias), GATE_DT), ssd_alog=_lane_row(f32(ssd_a_log), GATE_DT),
        ssd_dsk=row(jnp.repeat(ssd_d, SSD_HEADDIM, axis=-1)), ssd_norm=row(ssd_norm),
        mlstm_bi=_lane_row(f32(mlstm_b_i), GATE_I), mlstm_bf=_lane_row(f32(mlstm_b_f), GATE_F),
        w_out=w_out.astype(BF16), norm_post_mix=row(norm_post_mix),
        norm_pre_ffn=row(norm_pre_ffn), w_gate=w_gate.astype(BF16), w_up=w_up.astype(BF16),
        w_down=w_down.astype(BF16), norm_post_ffn=row(norm_post_ffn))


def _forward(x, *params, tt_in=512, tt_out=512, tc=32, nseq_ssd=4, nseq_mlstm=8, interpret=False):
    consts = dict(ssd_sel=_expand_matrix(GATE_DT, SSD_HEADS, SSD_HEADDIM),
                  mlstm_sel=_expand_matrix(GATE_I, MLSTM_HEADS, MLSTM_QKDIM))
    for layer in range(params[0].shape[0]):
        p = _prepare_layer(*(a[layer] for a in params))
        x = _layer(x, p, consts, in_place=layer > 0, tt_in=tt_in, tt_out=tt_out, tc=tc,
                   nseq_ssd=nseq_ssd, nseq_mlstm=nseq_mlstm, interpret=interpret)
    return x


def kernel(x, norm_pre_mix, w_in, s5_lambda_re, s5_lambda_im, s5_log_step, s5_b_re, s5_b_im, s5_c_re, s5_c_im, s5_d, s5_w_glu, s5_b_glu, ssd_conv_w, ssd_conv_b, ssd_dt_bias, ssd_a_log, ssd_d, ssd_norm, mlstm_conv_w, mlstm_conv_b, mlstm_b_i, mlstm_b_f, w_out, norm_post_mix, norm_pre_ffn, w_gate, w_up, w_down, norm_post_ffn):
    return _forward(x, norm_pre_mix, w_in, s5_lambda_re, s5_lambda_im, s5_log_step, s5_b_re,
                    s5_b_im, s5_c_re, s5_c_im, s5_d, s5_w_glu, s5_b_glu, ssd_conv_w, ssd_conv_b,
                    ssd_dt_bias, ssd_a_log, ssd_d, ssd_norm, mlstm_conv_w, mlstm_conv_b, mlstm_b_i,
                    mlstm_b_f, w_out, norm_post_mix, norm_pre_ffn, w_gate, w_up, w_down,
                    norm_post_ffn)
```

```python
import functools
import math

import jax
import jax.numpy as jnp
from jax import lax
from jax.experimental import pallas as pl
from jax.experimental.pallas import tpu as pltpu

F32 = jnp.float32
BF16 = jnp.bfloat16

D_MODEL = 1024
D_MIX = 2048
S5_WIDTH = 512
S5_GROUP = 16
S5_GROUPS = 32
S5_STATE = 64
S5_NSTATE = S5_GROUPS * S5_STATE
SSD_HEADDIM = 64
SSD_WIDTH = 768
SSD_HEADS = 12
SSD_GROUPS = 2
SSD_HPG = SSD_HEADS // SSD_GROUPS
SSD_GW = SSD_WIDTH // SSD_GROUPS
SSD_STATE = 128
SSD_XBC = SSD_WIDTH + 2 * SSD_GROUPS * SSD_STATE
MLSTM_WIDTH = 768
MLSTM_HEADS = 6
MLSTM_VDIM = 128
MLSTM_QKDIM = 64
MLSTM_QK = 384
CONV_K = 4
D_FF = 2816
NORM_EPS = 1e-6
IN_SPLIT_SIZES = (S5_WIDTH, SSD_WIDTH, SSD_XBC, SSD_HEADS, 2 * MLSTM_QK, MLSTM_WIDTH,
                  MLSTM_HEADS, MLSTM_HEADS, MLSTM_WIDTH)

LANES = 128
CHUNK = 128
HALO = 8
LOG2E = math.log2(math.e)
GATE_DT = 0
GATE_I = SSD_HEADS
GATE_F = SSD_HEADS + MLSTM_HEADS
P_U = 0
P_Z = P_U + S5_WIDTH
P_CONV = P_Z + SSD_WIDTH
P_V = P_CONV + SSD_XBC + 2 * MLSTM_QK
P_O = P_V + MLSTM_WIDTH
P_G = P_O + MLSTM_WIDTH
D_INP = P_G + LANES
CONV_W = SSD_XBC + 2 * MLSTM_QK

VMEM_LIMIT = 56 * 1024 * 1024


def _silu(x):
    hx = 0.5 * x
    return hx + hx * jnp.tanh(hx)


def _sigmoid(x):
    return 0.5 + 0.5 * jnp.tanh(0.5 * x)


def _softplus(x):
    return jnp.maximum(x, 0.0) + jnp.log1p(jnp.exp(-jnp.abs(x)))


def _rms(x, g):
    return x * lax.rsqrt(jnp.mean(x * x, axis=-1, keepdims=True) + NORM_EPS) * g


def _dot(a, b):
    return jnp.dot(a, b, preferred_element_type=F32)


def _dot_nt(a, b):
    return lax.dot_general(a, b, (((1,), (1,)), ((), ())), preferred_element_type=F32)


def _dot_tn(a, b):
    return lax.dot_general(a, b, (((0,), (0,)), ((), ())), preferred_element_type=F32)


def _cumsum_rows(x):
    t = x.shape[0]
    r = lax.broadcasted_iota(jnp.int32, (t, t), 0)
    c = lax.broadcasted_iota(jnp.int32, (t, t), 1)
    tril = jnp.where(c <= r, 1.0, 0.0).astype(F32)
    return jnp.dot(tril, x, precision=lax.Precision.HIGHEST, preferred_element_type=F32)


def _cummax_rows_blocked(x, blk):
    pos = lax.broadcasted_iota(jnp.int32, x.shape, 0) % blk
    d = 1
    while d < blk:
        x = jnp.maximum(x, jnp.where(pos >= d, pltpu.roll(x, d, axis=0), -jnp.inf))
        d *= 2
    return x


def _split3(x):
    hi = x.astype(BF16)
    r1 = x - hi.astype(F32)
    mid = r1.astype(BF16)
    lo = (r1 - mid.astype(F32)).astype(BF16)
    return hi, mid, lo


def _expand_matrix(first, heads, width):
    src = jnp.arange(LANES)[:, None] - first
    dst = jnp.arange(heads * width)[None, :]
    sel = ((dst >= src * width) & (dst < (src + 1) * width)).astype(BF16)
    return jnp.concatenate([sel, sel, sel], axis=0)


def _expand_heads(v, sel3_ref):
    return _dot(jnp.concatenate(_split3(v), axis=1), sel3_ref[...])


def _half_masks(t, width):
    lane = lax.broadcasted_iota(jnp.int32, (t, LANES), 1)
    lo = jnp.where(lane < width, 1.0, 0.0).astype(BF16)
    hi = jnp.where(lane >= width, 1.0, 0.0).astype(BF16)
    return lo, hi


CONV_CW = 256


def _shift_rows(x3, prev3, j):
    rolled = pltpu.roll(jnp.concatenate([prev3, x3], axis=0), j, axis=1)
    sub = lax.broadcasted_iota(jnp.int32, x3.shape, 1)
    return jnp.where(sub >= j, rolled[1:], rolled[:-1])


def _causal_conv4(x3, w_ref, cs):
    zero = jnp.zeros((1,) + x3.shape[1:], F32)
    x1 = _shift_rows(x3, zero, 1)
    a = w_ref[3:4, cs] * x3[1:] + w_ref[2:3, cs] * x1[1:]
    b = w_ref[1:2, cs] * x3 + w_ref[0:1, cs] * x1
    return a + _shift_rows(b[1:], b[0:1], 2)


def _inproj_kernel(x_ref, g_ref, w_ref, cw_ref, cb_ref, cs_ref,
                   u_ref, zx_ref, qk_ref, v_ref, so_ref, gt_ref, carry):
    t = x_ref.shape[1]

    @pl.when(pl.program_id(1) == 0)
    def _():
        carry[...] = jnp.zeros_like(carry)

    h = _rms(x_ref[0], g_ref[...]).astype(BF16)
    u_ref[0] = _dot(h, w_ref[:, P_U:P_Z])
    zx_ref[0, :, 0:SSD_WIDTH] = _silu(_dot(h, w_ref[:, P_Z:P_CONV]))
    for c in range(CONV_W // CONV_CW):
        cs = slice(c * CONV_CW, (c + 1) * CONV_CW)
        pc = _dot(h, w_ref[:, P_CONV + c * CONV_CW:P_CONV + (c + 1) * CONV_CW])
        x3 = jnp.concatenate([carry[:, cs], pc], axis=0).reshape(t // HALO + 1, HALO, CONV_CW)
        carry[:, cs] = pc[t - HALO:, :]
        acc = _causal_conv4(x3, cw_ref, cs).reshape(t, CONV_CW) + cb_ref[:, cs]
        act = _silu(acc)
        if (c + 1) * CONV_CW > SSD_XBC + MLSTM_QK:
            act = act * cs_ref[:, cs]
        if (c + 1) * CONV_CW <= SSD_XBC:
            zx_ref[0, :, SSD_WIDTH + c * CONV_CW:SSD_WIDTH + (c + 1) * CONV_CW] = act
        else:
            qk_ref[0, :, c * CONV_CW - SSD_XBC:(c + 1) * CONV_CW - SSD_XBC] = act
    v_ref[0] = _dot(h, w_ref[:, P_V:P_O]).astype(BF16)
    so_ref[0] = _sigmoid(_dot(h, w_ref[:, P_O:P_G]))
    gt_ref[0] = _dot(h, w_ref[:, P_G:D_INP])


def _in_proj(x, g, w, conv_w, conv_b, conv_scale, *, tt, interpret=False):
    b, l, d = x.shape
    tok = lambda width: pl.BlockSpec((1, tt, width), lambda i, j: (i, j, 0))
    const = lambda a: pl.BlockSpec(a.shape, lambda i, j, _n=a.ndim: (0,) * _n)
    sds = lambda width, dt: jax.ShapeDtypeStruct((b, l, width), dt)
    return pl.pallas_call(
        _inproj_kernel,
        grid=(b, l // tt),
        in_specs=[tok(d), const(g), const(w), const(conv_w), const(conv_b), const(conv_scale)],
        out_specs=[tok(S5_WIDTH), tok(SSD_WIDTH + SSD_XBC), tok(2 * MLSTM_QK), tok(MLSTM_WIDTH),
                   tok(MLSTM_WIDTH), tok(LANES)],
        out_shape=[sds(S5_WIDTH, F32), sds(SSD_WIDTH + SSD_XBC, F32), sds(2 * MLSTM_QK, F32),
                   sds(MLSTM_WIDTH, BF16), sds(MLSTM_WIDTH, F32), sds(LANES, F32)],
        scratch_shapes=[pltpu.VMEM((HALO, CONV_W), F32)],
        compiler_params=pltpu.CompilerParams(
            dimension_semantics=("parallel", "arbitrary"), vmem_limit_bytes=VMEM_LIMIT),
        name="in_proj",
        interpret=interpret,
    )(x, g, w, conv_w, conv_b, conv_scale)


S5_KB = 2
S5_KW = S5_WIDTH // S5_KB
S5_HALF = S5_NSTATE // S5_KB
S5_STRIP = 256


def _s5_kernel(u_ref, bblk_ref, cblk_ref, are_ref, aim_ref, d_ref, wglu_ref, bglu_ref,
               y_ref, bu_ref, xs_ref, st_ref, *, nb, tc):
    @pl.when(pl.program_id(0) == 0)
    def _():
        st_ref[...] = jnp.zeros_like(st_ref)

    m = nb * tc
    u = pltpu.einshape("btd->tbd", u_ref[...]).reshape(m, S5_WIDTH)
    ub = u.astype(BF16)
    ys = []
    for kb in range(S5_KB):
        ukb = ub[:, kb * S5_KW:(kb + 1) * S5_KW]
        ykb = None
        for s in range(S5_HALF // S5_STRIP):
            c_re = kb * 2 * S5_HALF + s * S5_STRIP
            c_im = c_re + S5_HALF
            c_a = kb * S5_HALF + s * S5_STRIP
            re_cols = slice(c_re, c_re + S5_STRIP)
            im_cols = slice(c_im, c_im + S5_STRIP)
            w_re = slice(s * S5_STRIP, (s + 1) * S5_STRIP)
            w_im = slice(S5_HALF + s * S5_STRIP, S5_HALF + (s + 1) * S5_STRIP)
            bu_ref[:, re_cols] = _dot(ukb, bblk_ref[kb, :, w_re])
            bu_ref[:, im_cols] = _dot(ukb, bblk_ref[kb, :, w_im])
            ar = jnp.broadcast_to(are_ref[:, c_a:c_a + S5_STRIP], (nb, S5_STRIP))
            ai = jnp.broadcast_to(aim_ref[:, c_a:c_a + S5_STRIP], (nb, S5_STRIP))
            xr = st_ref[:, re_cols]
            xi = st_ref[:, im_cols]
            for t in range(tc):
                rows = slice(t * nb, (t + 1) * nb)
                xr, xi = (ar * xr - ai * xi + bu_ref[rows, re_cols],
                          ar * xi + ai * xr + bu_ref[rows, im_cols])
                xs_ref[rows, re_cols] = xr.astype(BF16)
                xs_ref[rows, im_cols] = xi.astype(BF16)
            st_ref[:, re_cols] = xr
            st_ref[:, im_cols] = xi
            part = (_dot(xs_ref[:, re_cols], cblk_ref[kb, w_re, :])
                    + _dot(xs_ref[:, im_cols], cblk_ref[kb, w_im, :]))
            ykb = part if ykb is None else ykb + part
        ys.append(ykb)
    y = jnp.concatenate(ys, axis=1) + d_ref[...] * u
    g = jax.nn.gelu(y)
    out = g * _sigmoid(_dot(g.astype(BF16), wglu_ref[...]) + bglu_ref[...])
    y_ref[...] = pltpu.einshape("tbd->btd", out.reshape(tc, nb, S5_WIDTH)).astype(BF16)


def _s5(u, bblk, cblk, a_re, a_im, d_row, w_glu, b_glu, *, tc, interpret=False):
    nb, l, _ = u.shape
    m = nb * tc
    full = lambda a: pl.BlockSpec(a.shape, lambda i, _n=a.ndim: (0,) * _n)
    blk = pl.BlockSpec((nb, tc, S5_WIDTH), lambda i: (0, i, 0))
    return pl.pallas_call(
        functools.partial(_s5_kernel, nb=nb, tc=tc),
        grid=(l // tc,),
        in_specs=[blk, full(bblk), full(cblk), full(a_re), full(a_im), full(d_row),
                  full(w_glu), full(b_glu)],
        out_specs=blk,
        out_shape=jax.ShapeDtypeStruct((nb, l, S5_WIDTH), BF16),
        scratch_shapes=[pltpu.VMEM((m, 2 * S5_NSTATE), F32),
                        pltpu.VMEM((m, 2 * S5_NSTATE), BF16),
                        pltpu.VMEM((nb, 2 * S5_NSTATE), F32)],
        compiler_params=pltpu.CompilerParams(
            dimension_semantics=("arbitrary",), vmem_limit_bytes=VMEM_LIMIT),
        name="s5",
        interpret=interpret,
    )(u, bblk, cblk, a_re, a_im, d_row, w_glu, b_glu)


def _s5_prepare(lam_re, lam_im, log_step, b_re, b_im, c_re, c_im):
    step = jnp.exp(log_step)[:, None]
    mag = jnp.exp(lam_re * step)
    ang = lam_im * step
    a_re = mag * jnp.cos(ang)
    a_im = mag * jnp.sin(ang)
    den = lam_re * lam_re + lam_im * lam_im
    nr = a_re - 1.0
    f_re = (nr * lam_re + a_im * lam_im) / den
    f_im = (a_im * lam_re - nr * lam_im) / den
    bb_re = f_re[..., None] * b_re - f_im[..., None] * b_im
    bb_im = f_re[..., None] * b_im + f_im[..., None] * b_re
    gl = S5_GROUPS // S5_KB
    eye = jnp.eye(gl, dtype=F32)

    def b_block(bb):
        bb = bb.reshape(S5_KB, gl, S5_STATE, S5_GROUP)
        blk = jnp.einsum('kgph,gq->kghqp', bb, eye)
        return blk.reshape(S5_KB, gl * S5_GROUP, gl * S5_STATE)

    def c_block(cc):
        cc = cc.reshape(S5_KB, gl, S5_GROUP, S5_STATE)
        blk = jnp.einsum('kghp,gq->kgpqh', cc, eye)
        return blk.reshape(S5_KB, gl * S5_STATE, gl * S5_GROUP)

    bblk = jnp.concatenate([b_block(bb_re), b_block(bb_im)], axis=2).astype(BF16)
    cblk = jnp.concatenate([c_block(c_re), -c_block(c_im)], axis=1).astype(BF16)
    return bblk, cblk, a_re.reshape(1, S5_NSTATE), a_im.reshape(1, S5_NSTATE)


def _ssd_kernel(zx_ref, gt_ref, dtb_ref, alog_ref, dsk_ref, ng_ref, sel_ref, y_ref, hst):
    ns = zx_ref.shape[0]
    t = CHUNK
    rows = ns * t

    @pl.when(pl.program_id(1) == 0)
    def _():
        hst[...] = jnp.zeros_like(hst)

    seq = lambda a, sq: a[sq * t:(sq + 1) * t]
    stack = lambda parts: jnp.concatenate(parts, axis=0)
    per_seq = lambda vals: stack([jnp.broadcast_to(v, (t, v.shape[1])) for v in vals])
    col_blk = lambda a, lo, n: a[:, :, lo:lo + n].reshape(rows, n)

    xs = col_blk(zx_ref, SSD_WIDTH, SSD_WIDTH)
    dt = _softplus(gt_ref[...].reshape(rows, LANES) + dtb_ref[...])
    dta = dt * (-jnp.exp(alog_ref[...]))
    acs = stack([_cumsum_rows(seq(dta, sq)) for sq in range(ns)]) * LOG2E
    acs_t = [seq(acs, sq).T for sq in range(ns)]
    dt_e = _expand_heads(dt, sel_ref)
    acs_e = _expand_heads(acs, sel_ref)
    last = [acs_e[(sq + 1) * t - 1:(sq + 1) * t, :] for sq in range(ns)]
    xdt = xs * dt_e
    xdt_b = xdt.astype(BF16)
    xw_b = (xdt * jnp.exp2(per_seq(last) - acs_e)).astype(BF16)
    eacs_e = jnp.exp2(acs_e)
    cdec = [jnp.exp2(v) for v in last]

    pos = lax.broadcasted_iota(jnp.int32, (rows, t), 0) % t
    causal = lax.broadcasted_iota(jnp.int32, (rows, t), 1) <= pos
    masks = _half_masks(rows, SSD_HEADDIM)
    ys = []
    for g in range(SSD_GROUPS):
        b0 = 2 * SSD_WIDTH + g * SSD_STATE
        c0 = 2 * SSD_WIDTH + SSD_GROUPS * SSD_STATE + g * SSD_STATE
        bm = col_blk(zx_ref, b0, SSD_STATE).astype(BF16)
        cm = col_blk(zx_ref, c0, SSD_STATE).astype(BF16)
        cbm = stack([_dot_nt(seq(cm, sq), seq(bm, sq)) for sq in range(ns)])
        gs = slice(g * SSD_GW, (g + 1) * SSD_GW)
        h_prev = [hst[sq, g] for sq in range(ns)]
        y_g = stack([_dot(seq(cm, sq), h_prev[sq].astype(BF16)) for sq in range(ns)]) * eacs_e[:, gs]
        diag = []
        for pr in range(SSD_HPG // 2):
            ps = slice(g * SSD_GW + pr * LANES, g * SSD_GW + (pr + 1) * LANES)
            acc = None
            for q in range(2):
                j = g * SSD_HPG + pr * 2 + q
                a_row = stack([jnp.broadcast_to(acs_t[sq][j:j + 1, :], (t, t)) for sq in range(ns)])
                seg = acs[:, j:j + 1] - a_row
                m = (cbm * jnp.exp2(jnp.where(causal, seg, -jnp.inf))).astype(BF16)
                xm = xdt_b[:, ps] * masks[q]
                part = stack([_dot(seq(m, sq), seq(xm, sq)) for sq in range(ns)])
                acc = part if acc is None else acc + part
            diag.append(acc)
        ys.append(y_g + jnp.concatenate(diag, axis=1))
        for sq in range(ns):
            hst[sq, g] = h_prev[sq] * cdec[sq][:, gs] + _dot_tn(seq(bm, sq), seq(xw_b, sq)[:, gs])
    y = jnp.concatenate(ys, axis=1) + xs * dsk_ref[...]
    y = y * col_blk(zx_ref, 0, SSD_WIDTH)
    outs = []
    for g in range(SSD_GROUPS):
        gs = slice(g * SSD_GW, (g + 1) * SSD_GW)
        outs.append(_rms(y[:, gs], ng_ref[:, gs]))
    y_ref[...] = jnp.concatenate(outs, axis=1).astype(BF16).reshape(ns, t, SSD_WIDTH)


def _ssd(zx, gates, dtb, alog, dsk, ng, sel, *, nseq, interpret=False):
    b, l, _ = zx.shape
    full = lambda a: pl.BlockSpec(a.shape, lambda i, j, _n=a.ndim: (0,) * _n)
    tok = lambda width: pl.BlockSpec((nseq, CHUNK, width), lambda i, j: (i, j, 0))
    return pl.pallas_call(
        _ssd_kernel,
        grid=(b // nseq, l // CHUNK),
        in_specs=[tok(SSD_WIDTH + SSD_XBC), tok(LANES), full(dtb), full(alog), full(dsk), full(ng),
                  full(sel)],
        out_specs=tok(SSD_WIDTH),
        out_shape=jax.ShapeDtypeStruct((b, l, SSD_WIDTH), BF16),
        scratch_shapes=[pltpu.VMEM((nseq, SSD_GROUPS, SSD_STATE, SSD_GW), F32)],
        compiler_params=pltpu.CompilerParams(
            dimension_semantics=("parallel", "arbitrary"), vmem_limit_bytes=VMEM_LIMIT),
        name="ssd",
        interpret=interpret,
    )(zx, gates, dtb, alog, dsk, ng, sel)


def _mlstm_kernel(qk_ref, v_ref, so_ref, gt_ref, bi_ref, bf_ref, sel_ref, y_ref, cst, nst, mst):
    ns = qk_ref.shape[0]
    t = CHUNK
    rows = ns * t

    @pl.when(pl.program_id(1) == 0)
    def _():
        cst[...] = jnp.zeros_like(cst)
        nst[...] = jnp.zeros_like(nst)
        mst[...] = jnp.zeros_like(mst)

    seq = lambda a, sq: a[sq * t:(sq + 1) * t]
    last = lambda a, sq: a[(sq + 1) * t - 1:(sq + 1) * t]
    stack = lambda parts: jnp.concatenate(parts, axis=0)
    per_seq = lambda vals: stack([jnp.broadcast_to(v, (t, v.shape[1])) for v in vals])

    gates = gt_ref[...].reshape(rows, LANES)
    log_i = gates + bi_ref[...]
    log_f = -_softplus(-(gates + bf_ref[...]))
    log_f = pltpu.roll(log_f, LANES - (GATE_F - GATE_I), axis=1)
    bcum = stack([_cumsum_rows(seq(log_f, sq)) for sq in range(ns)])
    gvec = log_i - bcum
    cmax = _cummax_rows_blocked(gvec, t)
    g_tot = [last(bcum, sq) for sq in range(ns)]
    gmax = [last(cmax, sq) for sq in range(ns)]
    m_prev = [mst[sq] for sq in range(ns)]
    w_inter = bcum + per_seq(m_prev)
    m_row = jnp.maximum(w_inter, bcum + cmax)
    alpha2 = (bcum - m_row) * LOG2E
    s_int = jnp.exp(w_inter - m_row)
    enm = jnp.exp(-m_row)
    e_end = jnp.exp(gvec - per_seq(gmax))
    gvec2_t = [seq(gvec, sq).T * LOG2E for sq in range(ns)]
    s_old, s_new = [], []
    for sq in range(ns):
        a_loc = g_tot[sq] + gmax[sq]
        m_new = jnp.maximum(g_tot[sq] + m_prev[sq], a_loc)
        s_old.append(jnp.exp(g_tot[sq] + m_prev[sq] - m_new))
        s_new.append(jnp.exp(a_loc - m_new))
        mst[sq] = m_new

    q = qk_ref[:, :, 0:MLSTM_QK].reshape(rows, MLSTM_QK)
    k = qk_ref[:, :, MLSTM_QK:].reshape(rows, MLSTM_QK)
    qb = q.astype(BF16)
    kb = k.astype(BF16)
    e_end_b = e_end.astype(BF16)
    nrow = lax.broadcasted_iota(jnp.int32, (MLSTM_QK, LANES), 0)
    nlane = lax.broadcasted_iota(jnp.int32, (MLSTM_QK, LANES), 1)
    own = (nlane >= GATE_I) & ((nlane - GATE_I) * MLSTM_QKDIM <= nrow) & (
        nrow < (nlane - GATE_I + 1) * MLSTM_QKDIM)
    qn = []
    for sq in range(ns):
        n_prev = nst[sq]
        qn.append(_dot(seq(qb, sq), n_prev.astype(BF16)))
        n_loc = jnp.where(own, _dot_tn(seq(kb, sq), seq(e_end_b, sq)), 0.0)
        nst[sq] = s_old[sq] * n_prev + s_new[sq] * n_loc
    sqn = s_int * stack(qn)
    kwb = (k * _expand_heads(e_end, sel_ref)).astype(BF16)

    pos = lax.broadcasted_iota(jnp.int32, (rows, t), 0) % t
    causal = lax.broadcasted_iota(jnp.int32, (rows, t), 1) <= pos
    masks = _half_masks(rows, MLSTM_QKDIM)
    srow = lax.broadcasted_iota(jnp.int32, (LANES, MLSTM_VDIM), 0)
    for pr in range(MLSTM_HEADS // 2):
        ps = slice(pr * LANES, (pr + 1) * LANES)
        c_prev = [cst[sq, pr] for sq in range(ns)]
        c_prev_b = [c.astype(BF16) for c in c_prev]
        upd = [[None, None] for _ in range(ns)]
        for hq in range(2):
            h = pr * 2 + hq
            ln = GATE_I + h
            hs = slice(h * MLSTM_VDIM, (h + 1) * MLSTM_VDIM)
            q_hb = qb[:, ps] * masks[hq]
            s = stack([_dot_nt(seq(q_hb, sq), seq(kb, sq)[:, ps]) for sq in range(ns)])
            g_row = stack([jnp.broadcast_to(gvec2_t[sq][ln:ln + 1, :], (t, t)) for sq in range(ns)])
            dm = jnp.where(causal, alpha2[:, ln:ln + 1] + g_row, -jnp.inf)
            qs = s * jnp.exp2(dm)
            qsb = qs.astype(BF16)
            num = (stack([_dot(seq(qsb, sq), v_ref[sq, :, hs]) for sq in range(ns)])
                   + s_int[:, ln:ln + 1]
                   * stack([_dot(seq(q_hb, sq), c_prev_b[sq]) for sq in range(ns)]))
            den = jnp.sum(qs, axis=1, keepdims=True) + sqn[:, ln:ln + 1]
            den = jnp.maximum(jnp.abs(den), enm[:, ln:ln + 1])
            out = so_ref[:, :, hs].reshape(rows, MLSTM_VDIM) * (num / den)
            y_ref[:, :, hs] = out.astype(BF16).reshape(ns, t, MLSTM_VDIM)
            for sq in range(ns):
                upd[sq][hq] = _dot_tn(seq(kwb, sq)[:, ps], v_ref[sq, :, hs])
        lo = GATE_I + 2 * pr
        for sq in range(ns):
            so = jnp.where(srow < MLSTM_QKDIM, s_old[sq][:, lo:lo + 1], s_old[sq][:, lo + 1:lo + 2])
            sn = jnp.where(srow < MLSTM_QKDIM, s_new[sq][:, lo:lo + 1], s_new[sq][:, lo + 1:lo + 2])
            cst[sq, pr] = so * c_prev[sq] + sn * jnp.where(srow < MLSTM_QKDIM, upd[sq][0], upd[sq][1])


def _mlstm(qk, v, so, gates, bi, bf, sel, *, nseq, interpret=False):
    b, l, _ = qk.shape
    full = lambda a: pl.BlockSpec(a.shape, lambda i, j, _n=a.ndim: (0,) * _n)
    tok = lambda width: pl.BlockSpec((nseq, CHUNK, width), lambda i, j: (i, j, 0))
    return pl.pallas_call(
        _mlstm_kernel,
        grid=(b // nseq, l // CHUNK),
        in_specs=[tok(2 * MLSTM_QK), tok(MLSTM_WIDTH), tok(MLSTM_WIDTH), tok(LANES),
                  full(bi), full(bf), full(sel)],
        out_specs=tok(MLSTM_WIDTH),
        out_shape=jax.ShapeDtypeStruct((b, l, MLSTM_WIDTH), BF16),
        scratch_shapes=[pltpu.VMEM((nseq, MLSTM_HEADS // 2, LANES, MLSTM_VDIM), F32),
                        pltpu.VMEM((nseq, MLSTM_QK, LANES), F32),
                        pltpu.VMEM((nseq, 1, LANES), F32)],
        compiler_params=pltpu.CompilerParams(
            dimension_semantics=("parallel", "arbitrary"), vmem_limit_bytes=VMEM_LIMIT),
        name="mlstm",
        interpret=interpret,
    )(qk, v, so, gates, bi, bf, sel)


def _outffn_kernel(x_ref, ya_ref, yb_ref, yc_ref, wo_ref, g1_ref, g2_ref, wg_ref, wu_ref,
                   wd_ref, g3_ref, o_ref):
    a0 = S5_WIDTH
    a1 = S5_WIDTH + SSD_WIDTH
    mix = (_dot(ya_ref[0], wo_ref[0:a0, :]) + _dot(yb_ref[0], wo_ref[a0:a1, :])
           + _dot(yc_ref[0], wo_ref[a1:, :]))
    x1 = x_ref[0] + _rms(mix, g1_ref[...])
    h = _rms(x1, g2_ref[...]).astype(BF16)
    act = (_silu(_dot(h, wg_ref[...])) * _dot(h, wu_ref[...])).astype(BF16)
    ff = _dot(act, wd_ref[...])
    o_ref[0] = x1 + _rms(ff, g3_ref[...])


def _out_ffn(x, ya, yb, yc, wo, g1, g2, wg, wu, wd, g3, *, tt, in_place, interpret=False):
    b, l, d = x.shape
    tok = lambda width: pl.BlockSpec((1, tt, width), lambda i, j: (i, j, 0))
    const = lambda a: pl.BlockSpec(a.shape, lambda i, j, _n=a.ndim: (0,) * _n,
                                   pipeline_mode=pl.Buffered(1))
    return pl.pallas_call(
        _outffn_kernel,
        grid=(b, l // tt),
        in_specs=[tok(d), tok(S5_WIDTH), tok(SSD_WIDTH), tok(MLSTM_WIDTH), const(wo), const(g1),
                  const(g2), const(wg), const(wu), const(wd), const(g3)],
        out_specs=tok(d),
        out_shape=jax.ShapeDtypeStruct((b, l, d), F32),
        input_output_aliases={0: 0} if in_place else {},
        compiler_params=pltpu.CompilerParams(
            dimension_semantics=("parallel", "parallel"), vmem_limit_bytes=VMEM_LIMIT),
        name="out_ffn",
        interpret=interpret,
    )(x, ya, yb, yc, wo, g1, g2, wg, wu, wd, g3)


CAST_ROW_STEPS = 8


def _cast_kernel(w_ref, o_ref):
    o_ref[...] = w_ref[...].astype(o_ref.dtype)


def _layer_weight_bf16(w_stacked, layer, *, interpret=False):
    _, r, c = w_stacked.shape
    br = r // CAST_ROW_STEPS
    return pl.pallas_call(
        _cast_kernel,
        grid=(CAST_ROW_STEPS,),
        in_specs=[pl.BlockSpec((None, br, c), lambda i: (layer, i, 0))],
        out_specs=pl.BlockSpec((br, c), lambda i: (i, 0)),
        out_shape=jax.ShapeDtypeStruct((r, c), BF16),
        name="cast_bf16",
        interpret=interpret,
    )(w_stacked)


def _lane_row(vals, offset):
    return jnp.zeros((1, LANES), F32).at[0, offset:offset + vals.shape[0]].set(vals)


def _pack_w_in(w_in):
    offs = [0]
    for s in IN_SPLIT_SIZES:
        offs.append(offs[-1] + s)
    seg = lambda i: w_in[..., offs[i]:offs[i + 1]]
    u, z, xbc, dt, qk, v, gi, gf, o = (seg(i) for i in range(9))
    pad = jnp.zeros(w_in.shape[:-1] + (LANES - SSD_HEADS - 2 * MLSTM_HEADS,), w_in.dtype)
    return jnp.concatenate([u, z, xbc, qk, v, o, dt, gi, gf, pad], axis=-1).astype(BF16)


def _layer(x, p, consts, *, in_place, tt_in, tt_out, tc, nseq_ssd, nseq_mlstm, interpret):
    u, zx, qk, v, so, gates = _in_proj(x, p["norm_pre_mix"], p["w_in"], p["conv_w"], p["conv_b"],
                                       p["conv_scale"], tt=tt_in, interpret=interpret)
    ya = _s5(u, p["s5_bblk"], p["s5_cblk"], p["s5_are"], p["s5_aim"],
             p["s5_d"], p["s5_w_glu"], p["s5_b_glu"], tc=tc, interpret=interpret)
    yb = _ssd(zx, gates, p["ssd_dtb"], p["ssd_alog"], p["ssd_dsk"], p["ssd_norm"],
              consts["ssd_sel"], nseq=nseq_ssd, interpret=interpret)
    yc = _mlstm(qk, v, so, gates, p["mlstm_bi"], p["mlstm_bf"], consts["mlstm_sel"],
                nseq=nseq_mlstm, interpret=interpret)
    return _out_ffn(x, ya, yb, yc, p["w_out"], p["norm_post_mix"],
                    p["norm_pre_ffn"], p["w_gate"], p["w_up"], p["w_down"], p["norm_post_ffn"],
                    tt=tt_out, in_place=in_place, interpret=interpret)


def _prepare_layer(norm_pre_mix, w_in, s5_lambda_re, s5_lambda_im, s5_log_step, s5_b_re, s5_b_im,
                   s5_c_re, s5_c_im, s5_d, s5_w_glu, s5_b_glu, ssd_conv_w, ssd_conv_b,
                   ssd_dt_bias, ssd_a_log, ssd_d, ssd_norm, mlstm_conv_w, mlstm_conv_b, mlstm_b_i,
                   mlstm_b_f, norm_post_mix, norm_pre_ffn, norm_post_ffn):
    f32 = lambda a: a.astype(F32)
    row = lambda a: f32(a).reshape(1, -1)
    bblk, cblk, a_re, a_im = _s5_prepare(
        f32(s5_lambda_re), f32(s5_lambda_im), f32(s5_log_step), f32(s5_b_re), f32(s5_b_im),
        f32(s5_c_re), f32(s5_c_im))
    conv_scale = jnp.concatenate(
        [jnp.ones((SSD_XBC + MLSTM_QK,), F32), jnp.full((MLSTM_QK,), MLSTM_QKDIM ** -0.5, F32)])
    return dict(
        norm_pre_mix=row(norm_pre_mix), w_in=_pack_w_in(w_in),
        conv_w=f32(jnp.concatenate([ssd_conv_w, mlstm_conv_w], axis=-1)),
        conv_b=row(jnp.concatenate([ssd_conv_b, mlstm_conv_b], axis=-1)),
        conv_scale=conv_scale.reshape(1, CONV_W),
        s5_bblk=bblk, s5_cblk=cblk, s5_are=a_re, s5_aim=a_im, s5_d=row(s5_d),
        s5_w_glu=s5_w_glu.astype(BF16), s5_b_glu=row(s5_b_glu),
        ssd_dtb=_lane_row(f32(ssd_dt_bias), GATE_DT), ssd_alog=_lane_row(f32(ssd_a_log), GATE_DT),
        ssd_dsk=row(jnp.repeat(ssd_d, SSD_HEADDIM, axis=-1)), ssd_norm=row(ssd_norm),
        mlstm_bi=_lane_row(f32(mlstm_b_i), GATE_I), mlstm_bf=_lane_row(f32(mlstm_b_f), GATE_F),
        norm_post_mix=row(norm_post_mix), norm_pre_ffn=row(norm_pre_ffn),
        norm_post_ffn=row(norm_post_ffn))


PARAM_NAMES = ("norm_pre_mix", "w_in", "s5_lambda_re", "s5_lambda_im", "s5_log_step", "s5_b_re",
               "s5_b_im", "s5_c_re", "s5_c_im", "s5_d", "s5_w_glu", "s5_b_glu", "ssd_conv_w",
               "ssd_conv_b", "ssd_dt_bias", "ssd_a_log", "ssd_d", "ssd_norm", "mlstm_conv_w",
               "mlstm_conv_b", "mlstm_b_i", "mlstm_b_f", "w_out", "norm_post_mix", "norm_pre_ffn",
               "w_gate", "w_up", "w_down", "norm_post_ffn")
MXU_WEIGHTS = ("w_out", "w_gate", "w_up", "w_down")


def _forward(x, *params, tt_in=512, tt_out=512, tc=32, nseq_ssd=4, nseq_mlstm=8, interpret=False):
    consts = dict(ssd_sel=_expand_matrix(GATE_DT, SSD_HEADS, SSD_HEADDIM),
                  mlstm_sel=_expand_matrix(GATE_I, MLSTM_HEADS, MLSTM_QKDIM))
    named = dict(zip(PARAM_NAMES, params, strict=True))
    for layer in range(named["w_in"].shape[0]):
        p = _prepare_layer(**{k: v[layer] for k, v in named.items() if k not in MXU_WEIGHTS})
        for k in MXU_WEIGHTS:
            p[k] = _layer_weight_bf16(named[k], layer, interpret=interpret)
        x = _layer(x, p, consts, in_place=layer > 0, tt_in=tt_in, tt_out=tt_out, tc=tc,
                   nseq_ssd=nseq_ssd, nseq_mlstm=nseq_mlstm, interpret=interpret)
    return x


def kernel(x, norm_pre_mix, w_in, s5_lambda_re, s5_lambda_im, s5_log_step, s5_b_re, s5_b_im, s5_c_re, s5_c_im, s5_d, s5_w_glu, s5_b_glu, ssd_conv_w, ssd_conv_b, ssd_dt_bias, ssd_a_log, ssd_d, ssd_norm, mlstm_conv_w, mlstm_conv_b, mlstm_b_i, mlstm_b_f, w_out, norm_post_mix, norm_pre_ffn, w_gate, w_up, w_down, norm_post_ffn):
    return _forward(x, norm_pre_mix, w_in, s5_lambda_re, s5_lambda_im, s5_log_step, s5_b_re,
                    s5_b_im, s5_c_re, s5_c_im, s5_d, s5_w_glu, s5_b_glu, ssd_conv_w, ssd_conv_b,
                    ssd_dt_bias, ssd_a_log, ssd_d, ssd_norm, mlstm_conv_w, mlstm_conv_b, mlstm_b_i,
                    mlstm_b_f, w_out, norm_post_mix, norm_pre_ffn, w_gate, w_up, w_down,
                    norm_post_ffn)
```

```python
import functools
import math

import jax
import jax.numpy as jnp
from jax import lax
from jax.experimental import pallas as pl
from jax.experimental.pallas import tpu as pltpu

F32 = jnp.float32
BF16 = jnp.bfloat16

D_MODEL = 1024
D_MIX = 2048
S5_WIDTH = 512
S5_GROUP = 16
S5_GROUPS = 32
S5_STATE = 64
S5_NSTATE = S5_GROUPS * S5_STATE
SSD_HEADDIM = 64
SSD_WIDTH = 768
SSD_HEADS = 12
SSD_GROUPS = 2
SSD_HPG = SSD_HEADS // SSD_GROUPS
SSD_GW = SSD_WIDTH // SSD_GROUPS
SSD_STATE = 128
SSD_XBC = SSD_WIDTH + 2 * SSD_GROUPS * SSD_STATE
MLSTM_WIDTH = 768
MLSTM_HEADS = 6
MLSTM_VDIM = 128
MLSTM_QKDIM = 64
MLSTM_QK = 384
CONV_K = 4
D_FF = 2816
NORM_EPS = 1e-6
IN_SPLIT_SIZES = (S5_WIDTH, SSD_WIDTH, SSD_XBC, SSD_HEADS, 2 * MLSTM_QK, MLSTM_WIDTH,
                  MLSTM_HEADS, MLSTM_HEADS, MLSTM_WIDTH)

LANES = 128
CHUNK = 128
HALO = 8
LOG2E = math.log2(math.e)
GATE_DT = 0
GATE_I = SSD_HEADS
GATE_F = SSD_HEADS + MLSTM_HEADS
P_U = 0
P_Z = P_U + S5_WIDTH
P_CONV = P_Z + SSD_WIDTH
P_V = P_CONV + SSD_XBC + 2 * MLSTM_QK
P_O = P_V + MLSTM_WIDTH
P_G = P_O + MLSTM_WIDTH
D_INP = P_G + LANES
CONV_W = SSD_XBC + 2 * MLSTM_QK

VMEM_LIMIT = 56 * 1024 * 1024


def _silu(x):
    hx = 0.5 * x
    return hx + hx * jnp.tanh(hx)


def _sigmoid(x):
    return 0.5 + 0.5 * jnp.tanh(0.5 * x)


def _softplus(x):
    return jnp.maximum(x, 0.0) + jnp.log1p(jnp.exp(-jnp.abs(x)))


def _rms(x, g):
    return x * lax.rsqrt(jnp.mean(x * x, axis=-1, keepdims=True) + NORM_EPS) * g


def _dot(a, b):
    return jnp.dot(a, b, preferred_element_type=F32)


def _dot_nt(a, b):
    return lax.dot_general(a, b, (((1,), (1,)), ((), ())), preferred_element_type=F32)


def _dot_tn(a, b):
    return lax.dot_general(a, b, (((0,), (0,)), ((), ())), preferred_element_type=F32)


def _cumsum_rows(x):
    t = x.shape[0]
    r = lax.broadcasted_iota(jnp.int32, (t, t), 0)
    c = lax.broadcasted_iota(jnp.int32, (t, t), 1)
    tril = jnp.where(c <= r, 1.0, 0.0).astype(F32)
    return jnp.dot(tril, x, precision=lax.Precision.HIGHEST, preferred_element_type=F32)


def _cummax_rows_blocked(x, blk):
    pos = lax.broadcasted_iota(jnp.int32, x.shape, 0) % blk
    d = 1
    while d < blk:
        x = jnp.maximum(x, jnp.where(pos >= d, pltpu.roll(x, d, axis=0), -jnp.inf))
        d *= 2
    return x


def _split3(x):
    hi = x.astype(BF16)
    r1 = x - hi.astype(F32)
    mid = r1.astype(BF16)
    lo = (r1 - mid.astype(F32)).astype(BF16)
    return hi, mid, lo


def _expand_matrix(first, heads, width):
    src = jnp.arange(LANES)[:, None] - first
    dst = jnp.arange(heads * width)[None, :]
    sel = ((dst >= src * width) & (dst < (src + 1) * width)).astype(BF16)
    return jnp.concatenate([sel, sel, sel], axis=0)


def _expand_heads(v, sel3_ref):
    return _dot(jnp.concatenate(_split3(v), axis=1), sel3_ref[...])


def _half_masks(t, width):
    lane = lax.broadcasted_iota(jnp.int32, (t, LANES), 1)
    lo = jnp.where(lane < width, 1.0, 0.0).astype(BF16)
    hi = jnp.where(lane >= width, 1.0, 0.0).astype(BF16)
    return lo, hi


def _layer_spec(a, layer, **kw):
    zeros = (0,) * (a.ndim - 1)
    return pl.BlockSpec((None,) + a.shape[1:], lambda *_: (layer,) + zeros, **kw)


def _whole_spec(a):
    zeros = (0,) * a.ndim
    return pl.BlockSpec(a.shape, lambda *_: zeros)


CONV_CW = 256


def _shift_rows(x3, prev3, j):
    rolled = pltpu.roll(jnp.concatenate([prev3, x3], axis=0), j, axis=1)
    sub = lax.broadcasted_iota(jnp.int32, x3.shape, 1)
    return jnp.where(sub >= j, rolled[1:], rolled[:-1])


def _causal_conv4(x3, w_ref, cs):
    zero = jnp.zeros((1,) + x3.shape[1:], F32)
    x1 = _shift_rows(x3, zero, 1)
    a = w_ref[3:4, cs] * x3[1:] + w_ref[2:3, cs] * x1[1:]
    b = w_ref[1:2, cs] * x3 + w_ref[0:1, cs] * x1
    return a + _shift_rows(b[1:], b[0:1], 2)


def _inproj_kernel(x_ref, g_ref, w_ref, cw_ref, cb_ref, cs_ref,
                   u_ref, zx_ref, qk_ref, v_ref, so_ref, gt_ref, carry):
    t = x_ref.shape[1]

    @pl.when(pl.program_id(1) == 0)
    def _():
        carry[...] = jnp.zeros_like(carry)

    h = _rms(x_ref[0], g_ref[...]).astype(BF16)
    u_ref[0] = _dot(h, w_ref[:, P_U:P_Z])
    zx_ref[0, :, 0:SSD_WIDTH] = _silu(_dot(h, w_ref[:, P_Z:P_CONV]))
    for c in range(CONV_W // CONV_CW):
        cs = slice(c * CONV_CW, (c + 1) * CONV_CW)
        pc = _dot(h, w_ref[:, P_CONV + c * CONV_CW:P_CONV + (c + 1) * CONV_CW])
        x3 = jnp.concatenate([carry[:, cs], pc], axis=0).reshape(t // HALO + 1, HALO, CONV_CW)
        carry[:, cs] = pc[t - HALO:, :]
        acc = _causal_conv4(x3, cw_ref, cs).reshape(t, CONV_CW) + cb_ref[:, cs]
        act = _silu(acc)
        if (c + 1) * CONV_CW > SSD_XBC + MLSTM_QK:
            act = act * cs_ref[:, cs]
        if (c + 1) * CONV_CW <= SSD_XBC:
            zx_ref[0, :, SSD_WIDTH + c * CONV_CW:SSD_WIDTH + (c + 1) * CONV_CW] = act
        else:
            qk_ref[0, :, c * CONV_CW - SSD_XBC:(c + 1) * CONV_CW - SSD_XBC] = act
    v_ref[0] = _dot(h, w_ref[:, P_V:P_O]).astype(BF16)
    so_ref[0] = _sigmoid(_dot(h, w_ref[:, P_O:P_G]))
    gt_ref[0] = _dot(h, w_ref[:, P_G:D_INP])


def _in_proj(x, g, w, conv_w, conv_b, conv_scale, *, layer, tt, interpret=False):
    b, l, d = x.shape
    tok = lambda width: pl.BlockSpec((1, tt, width), lambda i, j: (i, j, 0))
    const = lambda a: _layer_spec(a, layer)
    sds = lambda width, dt: jax.ShapeDtypeStruct((b, l, width), dt)
    return pl.pallas_call(
        _inproj_kernel,
        grid=(b, l // tt),
        in_specs=[tok(d), const(g), const(w), const(conv_w), const(conv_b),
                  _whole_spec(conv_scale)],
        out_specs=[tok(S5_WIDTH), tok(SSD_WIDTH + SSD_XBC), tok(2 * MLSTM_QK), tok(MLSTM_WIDTH),
                   tok(MLSTM_WIDTH), tok(LANES)],
        out_shape=[sds(S5_WIDTH, F32), sds(SSD_WIDTH + SSD_XBC, F32), sds(2 * MLSTM_QK, F32),
                   sds(MLSTM_WIDTH, BF16), sds(MLSTM_WIDTH, F32), sds(LANES, F32)],
        scratch_shapes=[pltpu.VMEM((HALO, CONV_W), F32)],
        compiler_params=pltpu.CompilerParams(
            dimension_semantics=("parallel", "arbitrary"), vmem_limit_bytes=VMEM_LIMIT),
        name="in_proj",
        interpret=interpret,
    )(x, g, w, conv_w, conv_b, conv_scale)


S5_KB = 2
S5_KW = S5_WIDTH // S5_KB
S5_HALF = S5_NSTATE // S5_KB
S5_STRIP = 256


def _s5_kernel(u_ref, bblk_ref, cblk_ref, are_ref, aim_ref, d_ref, wglu_ref, bglu_ref,
               y_ref, bu_ref, xs_ref, st_ref, *, nb, tc):
    @pl.when(pl.program_id(0) == 0)
    def _():
        st_ref[...] = jnp.zeros_like(st_ref)

    m = nb * tc
    u = pltpu.einshape("btd->tbd", u_ref[...]).reshape(m, S5_WIDTH)
    ub = u.astype(BF16)
    ys = []
    for kb in range(S5_KB):
        ukb = ub[:, kb * S5_KW:(kb + 1) * S5_KW]
        ykb = None
        for s in range(S5_HALF // S5_STRIP):
            c_re = kb * 2 * S5_HALF + s * S5_STRIP
            c_im = c_re + S5_HALF
            c_a = kb * S5_HALF + s * S5_STRIP
            re_cols = slice(c_re, c_re + S5_STRIP)
            im_cols = slice(c_im, c_im + S5_STRIP)
            w_re = slice(s * S5_STRIP, (s + 1) * S5_STRIP)
            w_im = slice(S5_HALF + s * S5_STRIP, S5_HALF + (s + 1) * S5_STRIP)
            bu_ref[:, re_cols] = _dot(ukb, bblk_ref[kb, :, w_re])
            bu_ref[:, im_cols] = _dot(ukb, bblk_ref[kb, :, w_im])
            ar = jnp.broadcast_to(are_ref[:, c_a:c_a + S5_STRIP], (nb, S5_STRIP))
            ai = jnp.broadcast_to(aim_ref[:, c_a:c_a + S5_STRIP], (nb, S5_STRIP))
            xr = st_ref[:, re_cols]
            xi = st_ref[:, im_cols]
            for t in range(tc):
                rows = slice(t * nb, (t + 1) * nb)
                xr, xi = (ar * xr - ai * xi + bu_ref[rows, re_cols],
                          ar * xi + ai * xr + bu_ref[rows, im_cols])
                xs_ref[rows, re_cols] = xr.astype(BF16)
                xs_ref[rows, im_cols] = xi.astype(BF16)
            st_ref[:, re_cols] = xr
            st_ref[:, im_cols] = xi
            part = (_dot(xs_ref[:, re_cols], cblk_ref[kb, w_re, :])
                    + _dot(xs_ref[:, im_cols], cblk_ref[kb, w_im, :]))
            ykb = part if ykb is None else ykb + part
        ys.append(ykb)
    y = jnp.concatenate(ys, axis=1) + d_ref[...] * u
    g = jax.nn.gelu(y)
    out = g * _sigmoid(_dot(g.astype(BF16), wglu_ref[...]) + bglu_ref[...])
    y_ref[...] = pltpu.einshape("tbd->btd", out.reshape(tc, nb, S5_WIDTH)).astype(BF16)


def _s5(u, bblk, cblk, a_re, a_im, d_row, w_glu, b_glu, *, layer, tc, interpret=False):
    nb, l, _ = u.shape
    m = nb * tc
    full = lambda a: _layer_spec(a, layer)
    blk = pl.BlockSpec((nb, tc, S5_WIDTH), lambda i: (0, i, 0))
    return pl.pallas_call(
        functools.partial(_s5_kernel, nb=nb, tc=tc),
        grid=(l // tc,),
        in_specs=[blk, full(bblk), full(cblk), full(a_re), full(a_im), full(d_row),
                  full(w_glu), full(b_glu)],
        out_specs=blk,
        out_shape=jax.ShapeDtypeStruct((nb, l, S5_WIDTH), BF16),
        scratch_shapes=[pltpu.VMEM((m, 2 * S5_NSTATE), F32),
                        pltpu.VMEM((m, 2 * S5_NSTATE), BF16),
                        pltpu.VMEM((nb, 2 * S5_NSTATE), F32)],
        compiler_params=pltpu.CompilerParams(
            dimension_semantics=("arbitrary",), vmem_limit_bytes=VMEM_LIMIT),
        name="s5",
        interpret=interpret,
    )(u, bblk, cblk, a_re, a_im, d_row, w_glu, b_glu)


def _s5_prepare(lam_re, lam_im, log_step, b_re, b_im, c_re, c_im):
    step = jnp.exp(log_step)[:, None]
    mag = jnp.exp(lam_re * step)
    ang = lam_im * step
    a_re = mag * jnp.cos(ang)
    a_im = mag * jnp.sin(ang)
    den = lam_re * lam_re + lam_im * lam_im
    nr = a_re - 1.0
    f_re = (nr * lam_re + a_im * lam_im) / den
    f_im = (a_im * lam_re - nr * lam_im) / den
    bb_re = f_re[..., None] * b_re - f_im[..., None] * b_im
    bb_im = f_re[..., None] * b_im + f_im[..., None] * b_re
    gl = S5_GROUPS // S5_KB
    eye = jnp.eye(gl, dtype=F32)

    def b_block(bb):
        bb = bb.reshape(S5_KB, gl, S5_STATE, S5_GROUP)
        blk = jnp.einsum('kgph,gq->kghqp', bb, eye)
        return blk.reshape(S5_KB, gl * S5_GROUP, gl * S5_STATE)

    def c_block(cc):
        cc = cc.reshape(S5_KB, gl, S5_GROUP, S5_STATE)
        blk = jnp.einsum('kghp,gq->kgpqh', cc, eye)
        return blk.reshape(S5_KB, gl * S5_STATE, gl * S5_GROUP)

    bblk = jnp.concatenate([b_block(bb_re), b_block(bb_im)], axis=2).astype(BF16)
    cblk = jnp.concatenate([c_block(c_re), -c_block(c_im)], axis=1).astype(BF16)
    return bblk, cblk, a_re.reshape(1, S5_NSTATE), a_im.reshape(1, S5_NSTATE)


def _ssd_kernel(zx_ref, gt_ref, dtb_ref, alog_ref, dsk_ref, ng_ref, sel_ref, y_ref, hst):
    ns = zx_ref.shape[0]
    t = CHUNK
    rows = ns * t

    @pl.when(pl.program_id(1) == 0)
    def _():
        hst[...] = jnp.zeros_like(hst)

    seq = lambda a, sq: a[sq * t:(sq + 1) * t]
    stack = lambda parts: jnp.concatenate(parts, axis=0)
    per_seq = lambda vals: stack([jnp.broadcast_to(v, (t, v.shape[1])) for v in vals])
    col_blk = lambda a, lo, n: a[:, :, lo:lo + n].reshape(rows, n)

    xs = col_blk(zx_ref, SSD_WIDTH, SSD_WIDTH)
    dt = _softplus(gt_ref[...].reshape(rows, LANES) + dtb_ref[...])
    dta = dt * (-jnp.exp(alog_ref[...]))
    acs = stack([_cumsum_rows(seq(dta, sq)) for sq in range(ns)]) * LOG2E
    acs_t = [seq(acs, sq).T for sq in range(ns)]
    dt_e = _expand_heads(dt, sel_ref)
    acs_e = _expand_heads(acs, sel_ref)
    last = [acs_e[(sq + 1) * t - 1:(sq + 1) * t, :] for sq in range(ns)]
    xdt = xs * dt_e
    xdt_b = xdt.astype(BF16)
    xw_b = (xdt * jnp.exp2(per_seq(last) - acs_e)).astype(BF16)
    eacs_e = jnp.exp2(acs_e)
    cdec = [jnp.exp2(v) for v in last]

    pos = lax.broadcasted_iota(jnp.int32, (rows, t), 0) % t
    causal = lax.broadcasted_iota(jnp.int32, (rows, t), 1) <= pos
    masks = _half_masks(rows, SSD_HEADDIM)
    ys = []
    for g in range(SSD_GROUPS):
        b0 = 2 * SSD_WIDTH + g * SSD_STATE
        c0 = 2 * SSD_WIDTH + SSD_GROUPS * SSD_STATE + g * SSD_STATE
        bm = col_blk(zx_ref, b0, SSD_STATE).astype(BF16)
        cm = col_blk(zx_ref, c0, SSD_STATE).astype(BF16)
        cbm = stack([_dot_nt(seq(cm, sq), seq(bm, sq)) for sq in range(ns)])
        gs = slice(g * SSD_GW, (g + 1) * SSD_GW)
        h_prev = [hst[sq, g] for sq in range(ns)]
        y_g = stack([_dot(seq(cm, sq), h_prev[sq].astype(BF16)) for sq in range(ns)]) * eacs_e[:, gs]
        diag = []
        for pr in range(SSD_HPG // 2):
            ps = slice(g * SSD_GW + pr * LANES, g * SSD_GW + (pr + 1) * LANES)
            acc = None
            for q in range(2):
                j = g * SSD_HPG + pr * 2 + q
                a_row = stack([jnp.broadcast_to(acs_t[sq][j:j + 1, :], (t, t)) for sq in range(ns)])
                seg = acs[:, j:j + 1] - a_row
                m = (cbm * jnp.exp2(jnp.where(causal, seg, -jnp.inf))).astype(BF16)
                xm = xdt_b[:, ps] * masks[q]
                part = stack([_dot(seq(m, sq), seq(xm, sq)) for sq in range(ns)])
                acc = part if acc is None else acc + part
            diag.append(acc)
        ys.append(y_g + jnp.concatenate(diag, axis=1))
        for sq in range(ns):
            hst[sq, g] = h_prev[sq] * cdec[sq][:, gs] + _dot_tn(seq(bm, sq), seq(xw_b, sq)[:, gs])
    y = jnp.concatenate(ys, axis=1) + xs * dsk_ref[...]
    y = y * col_blk(zx_ref, 0, SSD_WIDTH)
    outs = []
    for g in range(SSD_GROUPS):
        gs = slice(g * SSD_GW, (g + 1) * SSD_GW)
        outs.append(_rms(y[:, gs], ng_ref[:, gs]))
    y_ref[...] = jnp.concatenate(outs, axis=1).astype(BF16).reshape(ns, t, SSD_WIDTH)


def _ssd(zx, gates, dtb, alog, dsk, ng, sel, *, layer, nseq, interpret=False):
    b, l, _ = zx.shape
    full = lambda a: _layer_spec(a, layer)
    tok = lambda width: pl.BlockSpec((nseq, CHUNK, width), lambda i, j: (i, j, 0))
    return pl.pallas_call(
        _ssd_kernel,
        grid=(b // nseq, l // CHUNK),
        in_specs=[tok(SSD_WIDTH + SSD_XBC), tok(LANES), full(dtb), full(alog), full(dsk), full(ng),
                  _whole_spec(sel)],
        out_specs=tok(SSD_WIDTH),
        out_shape=jax.ShapeDtypeStruct((b, l, SSD_WIDTH), BF16),
        scratch_shapes=[pltpu.VMEM((nseq, SSD_GROUPS, SSD_STATE, SSD_GW), F32)],
        compiler_params=pltpu.CompilerParams(
            dimension_semantics=("parallel", "arbitrary"), vmem_limit_bytes=VMEM_LIMIT),
        name="ssd",
        interpret=interpret,
    )(zx, gates, dtb, alog, dsk, ng, sel)


def _mlstm_kernel(qk_ref, v_ref, so_ref, gt_ref, bi_ref, bf_ref, sel_ref, y_ref, cst, nst, mst):
    ns = qk_ref.shape[0]
    t = CHUNK
    rows = ns * t

    @pl.when(pl.program_id(1) == 0)
    def _():
        cst[...] = jnp.zeros_like(cst)
        nst[...] = jnp.zeros_like(nst)
        mst[...] = jnp.zeros_like(mst)

    seq = lambda a, sq: a[sq * t:(sq + 1) * t]
    last = lambda a, sq: a[(sq + 1) * t - 1:(sq + 1) * t]
    stack = lambda parts: jnp.concatenate(parts, axis=0)
    per_seq = lambda vals: stack([jnp.broadcast_to(v, (t, v.shape[1])) for v in vals])

    gates = gt_ref[...].reshape(rows, LANES)
    log_i = gates + bi_ref[...]
    log_f = -_softplus(-(gates + bf_ref[...]))
    log_f = pltpu.roll(log_f, LANES - (GATE_F - GATE_I), axis=1)
    bcum = stack([_cumsum_rows(seq(log_f, sq)) for sq in range(ns)])
    gvec = log_i - bcum
    cmax = _cummax_rows_blocked(gvec, t)
    g_tot = [last(bcum, sq) for sq in range(ns)]
    gmax = [last(cmax, sq) for sq in range(ns)]
    m_prev = [mst[sq] for sq in range(ns)]
    w_inter = bcum + per_seq(m_prev)
    m_row = jnp.maximum(w_inter, bcum + cmax)
    alpha2 = (bcum - m_row) * LOG2E
    s_int = jnp.exp(w_inter - m_row)
    enm = jnp.exp(-m_row)
    e_end = jnp.exp(gvec - per_seq(gmax))
    gvec2_t = [seq(gvec, sq).T * LOG2E for sq in range(ns)]
    s_old, s_new = [], []
    for sq in range(ns):
        a_loc = g_tot[sq] + gmax[sq]
        m_new = jnp.maximum(g_tot[sq] + m_prev[sq], a_loc)
        s_old.append(jnp.exp(g_tot[sq] + m_prev[sq] - m_new))
        s_new.append(jnp.exp(a_loc - m_new))
        mst[sq] = m_new

    q = qk_ref[:, :, 0:MLSTM_QK].reshape(rows, MLSTM_QK)
    k = qk_ref[:, :, MLSTM_QK:].reshape(rows, MLSTM_QK)
    qb = q.astype(BF16)
    kb = k.astype(BF16)
    e_end_b = e_end.astype(BF16)
    nrow = lax.broadcasted_iota(jnp.int32, (MLSTM_QK, LANES), 0)
    nlane = lax.broadcasted_iota(jnp.int32, (MLSTM_QK, LANES), 1)
    own = (nlane >= GATE_I) & ((nlane - GATE_I) * MLSTM_QKDIM <= nrow) & (
        nrow < (nlane - GATE_I + 1) * MLSTM_QKDIM)
    qn = []
    for sq in range(ns):
        n_prev = nst[sq]
        qn.append(_dot(seq(qb, sq), n_prev.astype(BF16)))
        n_loc = jnp.where(own, _dot_tn(seq(kb, sq), seq(e_end_b, sq)), 0.0)
        nst[sq] = s_old[sq] * n_prev + s_new[sq] * n_loc
    sqn = s_int * stack(qn)
    kwb = (k * _expand_heads(e_end, sel_ref)).astype(BF16)

    pos = lax.broadcasted_iota(jnp.int32, (rows, t), 0) % t
    causal = lax.broadcasted_iota(jnp.int32, (rows, t), 1) <= pos
    masks = _half_masks(rows, MLSTM_QKDIM)
    srow = lax.broadcasted_iota(jnp.int32, (LANES, MLSTM_VDIM), 0)
    for pr in range(MLSTM_HEADS // 2):
        ps = slice(pr * LANES, (pr + 1) * LANES)
        c_prev = [cst[sq, pr] for sq in range(ns)]
        c_prev_b = [c.astype(BF16) for c in c_prev]
        upd = [[None, None] for _ in range(ns)]
        for hq in range(2):
            h = pr * 2 + hq
            ln = GATE_I + h
            hs = slice(h * MLSTM_VDIM, (h + 1) * MLSTM_VDIM)
            q_hb = qb[:, ps] * masks[hq]
            s = stack([_dot_nt(seq(q_hb, sq), seq(kb, sq)[:, ps]) for sq in range(ns)])
            g_row = stack([jnp.broadcast_to(gvec2_t[sq][ln:ln + 1, :], (t, t)) for sq in range(ns)])
            dm = jnp.where(causal, alpha2[:, ln:ln + 1] + g_row, -jnp.inf)
            qs = s * jnp.exp2(dm)
            qsb = qs.astype(BF16)
            num = (stack([_dot(seq(qsb, sq), v_ref[sq, :, hs]) for sq in range(ns)])
                   + s_int[:, ln:ln + 1]
                   * stack([_dot(seq(q_hb, sq), c_prev_b[sq]) for sq in range(ns)]))
            den = jnp.sum(qs, axis=1, keepdims=True) + sqn[:, ln:ln + 1]
            den = jnp.maximum(jnp.abs(den), enm[:, ln:ln + 1])
            out = so_ref[:, :, hs].reshape(rows, MLSTM_VDIM) * (num / den)
            y_ref[:, :, hs] = out.astype(BF16).reshape(ns, t, MLSTM_VDIM)
            for sq in range(ns):
                upd[sq][hq] = _dot_tn(seq(kwb, sq)[:, ps], v_ref[sq, :, hs])
        lo = GATE_I + 2 * pr
        for sq in range(ns):
            so = jnp.where(srow < MLSTM_QKDIM, s_old[sq][:, lo:lo + 1], s_old[sq][:, lo + 1:lo + 2])
            sn = jnp.where(srow < MLSTM_QKDIM, s_new[sq][:, lo:lo + 1], s_new[sq][:, lo + 1:lo + 2])
            cst[sq, pr] = so * c_prev[sq] + sn * jnp.where(srow < MLSTM_QKDIM, upd[sq][0], upd[sq][1])


def _mlstm(qk, v, so, gates, bi, bf, sel, *, layer, nseq, interpret=False):
    b, l, _ = qk.shape
    full = lambda a: _layer_spec(a, layer)
    tok = lambda width: pl.BlockSpec((nseq, CHUNK, width), lambda i, j: (i, j, 0))
    return pl.pallas_call(
        _mlstm_kernel,
        grid=(b // nseq, l // CHUNK),
        in_specs=[tok(2 * MLSTM_QK), tok(MLSTM_WIDTH), tok(MLSTM_WIDTH), tok(LANES),
                  full(bi), full(bf), _whole_spec(sel)],
        out_specs=tok(MLSTM_WIDTH),
        out_shape=jax.ShapeDtypeStruct((b, l, MLSTM_WIDTH), BF16),
        scratch_shapes=[pltpu.VMEM((nseq, MLSTM_HEADS // 2, LANES, MLSTM_VDIM), F32),
                        pltpu.VMEM((nseq, MLSTM_QK, LANES), F32),
                        pltpu.VMEM((nseq, 1, LANES), F32)],
        compiler_params=pltpu.CompilerParams(
            dimension_semantics=("parallel", "arbitrary"), vmem_limit_bytes=VMEM_LIMIT),
        name="mlstm",
        interpret=interpret,
    )(qk, v, so, gates, bi, bf, sel)


def _outffn_kernel(x_ref, ya_ref, yb_ref, yc_ref, wo_ref, g1_ref, g2_ref, wg_ref, wu_ref,
                   wd_ref, g3_ref, o_ref):
    a0 = S5_WIDTH
    a1 = S5_WIDTH + SSD_WIDTH
    mix = (_dot(ya_ref[0], wo_ref[0:a0, :]) + _dot(yb_ref[0], wo_ref[a0:a1, :])
           + _dot(yc_ref[0], wo_ref[a1:, :]))
    x1 = x_ref[0] + _rms(mix, g1_ref[...])
    h = _rms(x1, g2_ref[...]).astype(BF16)
    act = (_silu(_dot(h, wg_ref[...])) * _dot(h, wu_ref[...])).astype(BF16)
    ff = _dot(act, wd_ref[...])
    o_ref[0] = x1 + _rms(ff, g3_ref[...])


def _out_ffn(x, ya, yb, yc, wo, g1, g2, wg, wu, wd, g3, *, layer, tt, in_place, interpret=False):
    b, l, d = x.shape
    tok = lambda width: pl.BlockSpec((1, tt, width), lambda i, j: (i, j, 0))
    const = lambda a: _layer_spec(a, layer, pipeline_mode=pl.Buffered(1))
    return pl.pallas_call(
        _outffn_kernel,
        grid=(b, l // tt),
        in_specs=[tok(d), tok(S5_WIDTH), tok(SSD_WIDTH), tok(MLSTM_WIDTH), const(wo), const(g1),
                  const(g2), const(wg), const(wu), const(wd), const(g3)],
        out_specs=tok(d),
        out_shape=jax.ShapeDtypeStruct((b, l, d), F32),
        input_output_aliases={0: 0} if in_place else {},
        compiler_params=pltpu.CompilerParams(
            dimension_semantics=("parallel", "parallel"), vmem_limit_bytes=VMEM_LIMIT),
        name="out_ffn",
        interpret=interpret,
    )(x, ya, yb, yc, wo, g1, g2, wg, wu, wd, g3)


def _lane_row(vals, offset):
    return jnp.zeros((1, LANES), F32).at[0, offset:offset + vals.shape[0]].set(vals)


def _pack_w_in(w_in):
    offs = [0]
    for s in IN_SPLIT_SIZES:
        offs.append(offs[-1] + s)
    seg = lambda i: w_in[..., offs[i]:offs[i + 1]]
    u, z, xbc, dt, qk, v, gi, gf, o = (seg(i) for i in range(9))
    pad = jnp.zeros(w_in.shape[:-1] + (LANES - SSD_HEADS - 2 * MLSTM_HEADS,), w_in.dtype)
    return jnp.concatenate([u, z, xbc, qk, v, o, dt, gi, gf, pad], axis=-1).astype(BF16)


def _layer(x, p, consts, layer, *, tt_in, tt_out, tc, nseq_ssd, nseq_mlstm, interpret):
    u, zx, qk, v, so, gates = _in_proj(x, p["norm_pre_mix"], p["w_in"], p["conv_w"], p["conv_b"],
                                       consts["conv_scale"], layer=layer, tt=tt_in,
                                       interpret=interpret)
    ya = _s5(u, p["s5_bblk"], p["s5_cblk"], p["s5_are"], p["s5_aim"],
             p["s5_d"], p["s5_w_glu"], p["s5_b_glu"], layer=layer, tc=tc, interpret=interpret)
    yb = _ssd(zx, gates, p["ssd_dtb"], p["ssd_alog"], p["ssd_dsk"], p["ssd_norm"],
              consts["ssd_sel"], layer=layer, nseq=nseq_ssd, interpret=interpret)
    yc = _mlstm(qk, v, so, gates, p["mlstm_bi"], p["mlstm_bf"], consts["mlstm_sel"],
                layer=layer, nseq=nseq_mlstm, interpret=interpret)
    return _out_ffn(x, ya, yb, yc, p["w_out"], p["norm_post_mix"],
                    p["norm_pre_ffn"], p["w_gate"], p["w_up"], p["w_down"], p["norm_post_ffn"],
                    layer=layer, tt=tt_out, in_place=layer > 0, interpret=interpret)


def _prepare_params(norm_pre_mix, w_in, s5_lambda_re, s5_lambda_im, s5_log_step, s5_b_re, s5_b_im,
                    s5_c_re, s5_c_im, s5_d, s5_w_glu, s5_b_glu, ssd_conv_w, ssd_conv_b,
                    ssd_dt_bias, ssd_a_log, ssd_d, ssd_norm, mlstm_conv_w, mlstm_conv_b, mlstm_b_i,
                    mlstm_b_f, w_out, norm_post_mix, norm_pre_ffn, w_gate, w_up, w_down,
                    norm_post_ffn):
    depth = w_in.shape[0]
    f32 = lambda a: a.astype(F32)
    row = lambda a: f32(a).reshape(depth, 1, -1)
    lanes = lambda a, off: jax.vmap(lambda v: _lane_row(v, off))(f32(a))
    bblk, cblk, a_re, a_im = jax.vmap(_s5_prepare)(
        f32(s5_lambda_re), f32(s5_lambda_im), f32(s5_log_step), f32(s5_b_re), f32(s5_b_im),
        f32(s5_c_re), f32(s5_c_im))
    return dict(
        norm_pre_mix=row(norm_pre_mix), w_in=_pack_w_in(w_in),
        conv_w=f32(jnp.concatenate([ssd_conv_w, mlstm_conv_w], axis=-1)),
        conv_b=row(jnp.concatenate([ssd_conv_b, mlstm_conv_b], axis=-1)),
        s5_bblk=bblk, s5_cblk=cblk, s5_are=a_re, s5_aim=a_im, s5_d=row(s5_d),
        s5_w_glu=s5_w_glu.astype(BF16), s5_b_glu=row(s5_b_glu),
        ssd_dtb=lanes(ssd_dt_bias, GATE_DT), ssd_alog=lanes(ssd_a_log, GATE_DT),
        ssd_dsk=row(jnp.repeat(ssd_d, SSD_HEADDIM, axis=-1)), ssd_norm=row(ssd_norm),
        mlstm_bi=lanes(mlstm_b_i, GATE_I), mlstm_bf=lanes(mlstm_b_f, GATE_F),
        w_out=w_out.astype(BF16), norm_post_mix=row(norm_post_mix),
        norm_pre_ffn=row(norm_pre_ffn), w_gate=w_gate.astype(BF16), w_up=w_up.astype(BF16),
        w_down=w_down.astype(BF16), norm_post_ffn=row(norm_post_ffn))


def _forward(x, *params, tt_in=512, tt_out=512, tc=32, nseq_ssd=4, nseq_mlstm=8, interpret=False):
    conv_scale = jnp.concatenate(
        [jnp.ones((SSD_XBC + MLSTM_QK,), F32), jnp.full((MLSTM_QK,), MLSTM_QKDIM ** -0.5, F32)])
    consts = dict(ssd_sel=_expand_matrix(GATE_DT, SSD_HEADS, SSD_HEADDIM),
                  mlstm_sel=_expand_matrix(GATE_I, MLSTM_HEADS, MLSTM_QKDIM),
                  conv_scale=conv_scale.reshape(1, CONV_W))
    stacked = _prepare_params(*params)
    for layer in range(stacked["w_in"].shape[0]):
        x = _layer(x, stacked, consts, layer, tt_in=tt_in, tt_out=tt_out, tc=tc,
                   nseq_ssd=nseq_ssd, nseq_mlstm=nseq_mlstm, interpret=interpret)
    return x


def kernel(x, norm_pre_mix, w_in, s5_lambda_re, s5_lambda_im, s5_log_step, s5_b_re, s5_b_im, s5_c_re, s5_c_im, s5_d, s5_w_glu, s5_b_glu, ssd_conv_w, ssd_conv_b, ssd_dt_bias, ssd_a_log, ssd_d, ssd_norm, mlstm_conv_w, mlstm_conv_b, mlstm_b_i, mlstm_b_f, w_out, norm_post_mix, norm_pre_ffn, w_gate, w_up, w_down, norm_post_ffn):
    return _forward(x, norm_pre_mix, w_in, s5_lambda_re, s5_lambda_im, s5_log_step, s5_b_re,
                    s5_b_im, s5_c_re, s5_c_im, s5_d, s5_w_glu, s5_b_glu, ssd_conv_w, ssd_conv_b,
                    ssd_dt_bias, ssd_a_log, ssd_d, ssd_norm, mlstm_conv_w, mlstm_conv_b, mlstm_b_i,
                    mlstm_b_f, w_out, norm_post_mix, norm_pre_ffn, w_gate, w_up, w_down,
                    norm_post_ffn)
```

```python
import functools
import math

import jax
import jax.numpy as jnp
from jax import lax
from jax.experimental import pallas as pl
from jax.experimental.pallas import tpu as pltpu

F32 = jnp.float32
BF16 = jnp.bfloat16

D_MODEL = 1024
D_MIX = 2048
S5_WIDTH = 512
S5_GROUP = 16
S5_GROUPS = 32
S5_STATE = 64
S5_NSTATE = S5_GROUPS * S5_STATE
SSD_HEADDIM = 64
SSD_WIDTH = 768
SSD_HEADS = 12
SSD_GROUPS = 2
SSD_HPG = SSD_HEADS // SSD_GROUPS
SSD_GW = SSD_WIDTH // SSD_GROUPS
SSD_STATE = 128
SSD_XBC = SSD_WIDTH + 2 * SSD_GROUPS * SSD_STATE
MLSTM_WIDTH = 768
MLSTM_HEADS = 6
MLSTM_VDIM = 128
MLSTM_QKDIM = 64
MLSTM_QK = 384
CONV_K = 4
D_FF = 2816
NORM_EPS = 1e-6
IN_SPLIT_SIZES = (S5_WIDTH, SSD_WIDTH, SSD_XBC, SSD_HEADS, 2 * MLSTM_QK, MLSTM_WIDTH,
                  MLSTM_HEADS, MLSTM_HEADS, MLSTM_WIDTH)

LANES = 128
CHUNK = 128
HALO = 8
LOG2E = math.log2(math.e)
GATE_DT = 0
GATE_I = SSD_HEADS
GATE_F = SSD_HEADS + MLSTM_HEADS
P_U = 0
P_Z = P_U + S5_WIDTH
P_CONV = P_Z + SSD_WIDTH
P_V = P_CONV + SSD_XBC + 2 * MLSTM_QK
P_O = P_V + MLSTM_WIDTH
P_G = P_O + MLSTM_WIDTH
D_INP = P_G + LANES
CONV_W = SSD_XBC + 2 * MLSTM_QK

VMEM_LIMIT = 56 * 1024 * 1024


def _silu(x):
    hx = 0.5 * x
    return hx + hx * jnp.tanh(hx)


def _sigmoid(x):
    return 0.5 + 0.5 * jnp.tanh(0.5 * x)


def _softplus(x):
    return jnp.maximum(x, 0.0) + jnp.log1p(jnp.exp(-jnp.abs(x)))


def _rms(x, g):
    return x * lax.rsqrt(jnp.mean(x * x, axis=-1, keepdims=True) + NORM_EPS) * g


def _dot(a, b):
    return jnp.dot(a, b, preferred_element_type=F32)


def _dot_nt(a, b):
    return lax.dot_general(a, b, (((1,), (1,)), ((), ())), preferred_element_type=F32)


def _dot_tn(a, b):
    return lax.dot_general(a, b, (((0,), (0,)), ((), ())), preferred_element_type=F32)


def _cumsum_rows(x):
    t = x.shape[0]
    r = lax.broadcasted_iota(jnp.int32, (t, t), 0)
    c = lax.broadcasted_iota(jnp.int32, (t, t), 1)
    tril = jnp.where(c <= r, 1.0, 0.0).astype(F32)
    return jnp.dot(tril, x, precision=lax.Precision.HIGHEST, preferred_element_type=F32)


def _cummax_rows_blocked(x, blk):
    pos = lax.broadcasted_iota(jnp.int32, x.shape, 0) % blk
    d = 1
    while d < blk:
        x = jnp.maximum(x, jnp.where(pos >= d, pltpu.roll(x, d, axis=0), -jnp.inf))
        d *= 2
    return x


def _split3(x):
    hi = x.astype(BF16)
    r1 = x - hi.astype(F32)
    mid = r1.astype(BF16)
    lo = (r1 - mid.astype(F32)).astype(BF16)
    return hi, mid, lo


def _expand_matrix(first, heads, width):
    src = jnp.arange(LANES)[:, None] - first
    dst = jnp.arange(heads * width)[None, :]
    sel = ((dst >= src * width) & (dst < (src + 1) * width)).astype(BF16)
    return jnp.concatenate([sel, sel, sel], axis=0)


def _expand_heads(v, sel3_ref):
    return _dot(jnp.concatenate(_split3(v), axis=1), sel3_ref[...])


def _half_masks(t, width):
    lane = lax.broadcasted_iota(jnp.int32, (t, LANES), 1)
    lo = jnp.where(lane < width, 1.0, 0.0).astype(BF16)
    hi = jnp.where(lane >= width, 1.0, 0.0).astype(BF16)
    return lo, hi


def _layer_spec(a, layer, **kw):
    zeros = (0,) * (a.ndim - 1)
    return pl.BlockSpec((None,) + a.shape[1:], lambda *_: (layer,) + zeros, **kw)


def _whole_spec(a):
    zeros = (0,) * a.ndim
    return pl.BlockSpec(a.shape, lambda *_: zeros)


CONV_CW = 256


def _shift_rows(x3, prev3, j):
    rolled = pltpu.roll(jnp.concatenate([prev3, x3], axis=0), j, axis=1)
    sub = lax.broadcasted_iota(jnp.int32, x3.shape, 1)
    return jnp.where(sub >= j, rolled[1:], rolled[:-1])


def _causal_conv4(x3, w_ref, cs):
    zero = jnp.zeros((1,) + x3.shape[1:], F32)
    x1 = _shift_rows(x3, zero, 1)
    a = w_ref[3:4, cs] * x3[1:] + w_ref[2:3, cs] * x1[1:]
    b = w_ref[1:2, cs] * x3 + w_ref[0:1, cs] * x1
    return a + _shift_rows(b[1:], b[0:1], 2)


def _inproj_kernel(x_ref, g_ref, w_ref, cw_ref, cb_ref, cs_ref,
                   u_ref, zx_ref, qk_ref, v_ref, so_ref, gt_ref, carry):
    t = x_ref.shape[1]

    @pl.when(pl.program_id(1) == 0)
    def _():
        carry[...] = jnp.zeros_like(carry)

    h = _rms(x_ref[0], g_ref[...]).astype(BF16)
    u_ref[0] = _dot(h, w_ref[:, P_U:P_Z])
    zx_ref[0, :, 0:SSD_WIDTH] = _silu(_dot(h, w_ref[:, P_Z:P_CONV]))
    for c in range(CONV_W // CONV_CW):
        cs = slice(c * CONV_CW, (c + 1) * CONV_CW)
        pc = _dot(h, w_ref[:, P_CONV + c * CONV_CW:P_CONV + (c + 1) * CONV_CW])
        x3 = jnp.concatenate([carry[:, cs], pc], axis=0).reshape(t // HALO + 1, HALO, CONV_CW)
        carry[:, cs] = pc[t - HALO:, :]
        acc = _causal_conv4(x3, cw_ref, cs).reshape(t, CONV_CW) + cb_ref[:, cs]
        act = _silu(acc)
        if (c + 1) * CONV_CW > SSD_XBC + MLSTM_QK:
            act = act * cs_ref[:, cs]
        if (c + 1) * CONV_CW <= SSD_XBC:
            zx_ref[0, :, SSD_WIDTH + c * CONV_CW:SSD_WIDTH + (c + 1) * CONV_CW] = act
        else:
            qk_ref[0, :, c * CONV_CW - SSD_XBC:(c + 1) * CONV_CW - SSD_XBC] = act
    v_ref[0] = _dot(h, w_ref[:, P_V:P_O]).astype(BF16)
    so_ref[0] = _sigmoid(_dot(h, w_ref[:, P_O:P_G]))
    gt_ref[0] = _dot(h, w_ref[:, P_G:D_INP])


def _in_proj(x, g, w, conv_w, conv_b, conv_scale, *, layer, tt, interpret=False):
    b, l, d = x.shape
    tok = lambda width: pl.BlockSpec((1, tt, width), lambda i, j: (i, j, 0))
    const = lambda a: _layer_spec(a, layer)
    sds = lambda width, dt: jax.ShapeDtypeStruct((b, l, width), dt)
    return pl.pallas_call(
        _inproj_kernel,
        grid=(b, l // tt),
        in_specs=[tok(d), const(g), const(w), const(conv_w), const(conv_b),
                  _whole_spec(conv_scale)],
        out_specs=[tok(S5_WIDTH), tok(SSD_WIDTH + SSD_XBC), tok(2 * MLSTM_QK), tok(MLSTM_WIDTH),
                   tok(MLSTM_WIDTH), tok(LANES)],
        out_shape=[sds(S5_WIDTH, F32), sds(SSD_WIDTH + SSD_XBC, F32), sds(2 * MLSTM_QK, F32),
                   sds(MLSTM_WIDTH, BF16), sds(MLSTM_WIDTH, F32), sds(LANES, F32)],
        scratch_shapes=[pltpu.VMEM((HALO, CONV_W), F32)],
        compiler_params=pltpu.CompilerParams(
            dimension_semantics=("parallel", "arbitrary"), vmem_limit_bytes=VMEM_LIMIT),
        name="in_proj",
        interpret=interpret,
    )(x, g, w, conv_w, conv_b, conv_scale)


S5_KB = 2
S5_KW = S5_WIDTH // S5_KB
S5_HALF = S5_NSTATE // S5_KB
S5_STRIP = 256


def _s5_kernel(u_ref, bblk_ref, cblk_ref, are_ref, aim_ref, d_ref, wglu_ref, bglu_ref,
               y_ref, bu_ref, xs_ref, st_ref, *, nb, tc):
    @pl.when(pl.program_id(0) == 0)
    def _():
        st_ref[...] = jnp.zeros_like(st_ref)

    m = nb * tc
    u = pltpu.einshape("btd->tbd", u_ref[...]).reshape(m, S5_WIDTH)
    ub = u.astype(BF16)
    ys = []
    for kb in range(S5_KB):
        ukb = ub[:, kb * S5_KW:(kb + 1) * S5_KW]
        ykb = None
        for s in range(S5_HALF // S5_STRIP):
            c_re = kb * 2 * S5_HALF + s * S5_STRIP
            c_im = c_re + S5_HALF
            c_a = kb * S5_HALF + s * S5_STRIP
            re_cols = slice(c_re, c_re + S5_STRIP)
            im_cols = slice(c_im, c_im + S5_STRIP)
            w_re = slice(s * S5_STRIP, (s + 1) * S5_STRIP)
            w_im = slice(S5_HALF + s * S5_STRIP, S5_HALF + (s + 1) * S5_STRIP)
            bu_ref[:, re_cols] = _dot(ukb, bblk_ref[kb, :, w_re])
            bu_ref[:, im_cols] = _dot(ukb, bblk_ref[kb, :, w_im])
            ar = jnp.broadcast_to(are_ref[:, c_a:c_a + S5_STRIP], (nb, S5_STRIP))
            ai = jnp.broadcast_to(aim_ref[:, c_a:c_a + S5_STRIP], (nb, S5_STRIP))
            xr = st_ref[:, re_cols]
            xi = st_ref[:, im_cols]
            for t in range(tc):
                rows = slice(t * nb, (t + 1) * nb)
                xr, xi = (ar * xr - ai * xi + bu_ref[rows, re_cols],
                          ar * xi + ai * xr + bu_ref[rows, im_cols])
                xs_ref[rows, re_cols] = xr.astype(BF16)
                xs_ref[rows, im_cols] = xi.astype(BF16)
            st_ref[:, re_cols] = xr
            st_ref[:, im_cols] = xi
            part = (_dot(xs_ref[:, re_cols], cblk_ref[kb, w_re, :])
                    + _dot(xs_ref[:, im_cols], cblk_ref[kb, w_im, :]))
            ykb = part if ykb is None else ykb + part
        ys.append(ykb)
    y = jnp.concatenate(ys, axis=1) + d_ref[...] * u
    g = jax.nn.gelu(y)
    out = g * _sigmoid(_dot(g.astype(BF16), wglu_ref[...]) + bglu_ref[...])
    y_ref[...] = pltpu.einshape("tbd->btd", out.reshape(tc, nb, S5_WIDTH)).astype(BF16)


def _s5(u, bblk, cblk, a_re, a_im, d_row, w_glu, b_glu, *, layer, tc, interpret=False):
    nb, l, _ = u.shape
    m = nb * tc
    full = lambda a: _layer_spec(a, layer)
    blk = pl.BlockSpec((nb, tc, S5_WIDTH), lambda i: (0, i, 0))
    return pl.pallas_call(
        functools.partial(_s5_kernel, nb=nb, tc=tc),
        grid=(l // tc,),
        in_specs=[blk, full(bblk), full(cblk), full(a_re), full(a_im), full(d_row),
                  full(w_glu), full(b_glu)],
        out_specs=blk,
        out_shape=jax.ShapeDtypeStruct((nb, l, S5_WIDTH), BF16),
        scratch_shapes=[pltpu.VMEM((m, 2 * S5_NSTATE), F32),
                        pltpu.VMEM((m, 2 * S5_NSTATE), BF16),
                        pltpu.VMEM((nb, 2 * S5_NSTATE), F32)],
        compiler_params=pltpu.CompilerParams(
            dimension_semantics=("arbitrary",), vmem_limit_bytes=VMEM_LIMIT),
        name="s5",
        interpret=interpret,
    )(u, bblk, cblk, a_re, a_im, d_row, w_glu, b_glu)


def _s5_prepare(lam_re, lam_im, log_step, b_re, b_im, c_re, c_im):
    step = jnp.exp(log_step)[:, None]
    mag = jnp.exp(lam_re * step)
    ang = lam_im * step
    a_re = mag * jnp.cos(ang)
    a_im = mag * jnp.sin(ang)
    den = lam_re * lam_re + lam_im * lam_im
    nr = a_re - 1.0
    f_re = (nr * lam_re + a_im * lam_im) / den
    f_im = (a_im * lam_re - nr * lam_im) / den
    bb_re = f_re[..., None] * b_re - f_im[..., None] * b_im
    bb_im = f_re[..., None] * b_im + f_im[..., None] * b_re
    gl = S5_GROUPS // S5_KB
    eye = jnp.eye(gl, dtype=F32)

    def b_block(bb):
        bb = bb.reshape(S5_KB, gl, S5_STATE, S5_GROUP)
        blk = jnp.einsum('kgph,gq->kghqp', bb, eye)
        return blk.reshape(S5_KB, gl * S5_GROUP, gl * S5_STATE)

    def c_block(cc):
        cc = cc.reshape(S5_KB, gl, S5_GROUP, S5_STATE)
        blk = jnp.einsum('kghp,gq->kgpqh', cc, eye)
        return blk.reshape(S5_KB, gl * S5_STATE, gl * S5_GROUP)

    bblk = jnp.concatenate([b_block(bb_re), b_block(bb_im)], axis=2).astype(BF16)
    cblk = jnp.concatenate([c_block(c_re), -c_block(c_im)], axis=1).astype(BF16)
    return bblk, cblk, a_re.reshape(1, S5_NSTATE), a_im.reshape(1, S5_NSTATE)


def _ssd_kernel(zx_ref, gt_ref, dtb_ref, alog_ref, dsk_ref, ng_ref, sel_ref, y_ref, hst):
    ns = zx_ref.shape[0]
    t = CHUNK
    rows = ns * t

    @pl.when(pl.program_id(1) == 0)
    def _():
        hst[...] = jnp.zeros_like(hst)

    seq = lambda a, sq: a[sq * t:(sq + 1) * t]
    stack = lambda parts: jnp.concatenate(parts, axis=0)
    per_seq = lambda vals: stack([jnp.broadcast_to(v, (t, v.shape[1])) for v in vals])
    col_blk = lambda a, lo, n: a[:, :, lo:lo + n].reshape(rows, n)

    xs = col_blk(zx_ref, SSD_WIDTH, SSD_WIDTH)
    dt = _softplus(gt_ref[...].reshape(rows, LANES) + dtb_ref[...])
    dta = dt * (-jnp.exp(alog_ref[...]))
    acs = stack([_cumsum_rows(seq(dta, sq)) for sq in range(ns)]) * LOG2E
    acs_t = [seq(acs, sq).T for sq in range(ns)]
    dt_e = _expand_heads(dt, sel_ref)
    acs_e = _expand_heads(acs, sel_ref)
    last = [acs_e[(sq + 1) * t - 1:(sq + 1) * t, :] for sq in range(ns)]
    xdt = xs * dt_e
    xdt_b = xdt.astype(BF16)
    xw_b = (xdt * jnp.exp2(per_seq(last) - acs_e)).astype(BF16)
    eacs_e = jnp.exp2(acs_e)
    cdec = [jnp.exp2(v) for v in last]

    pos = lax.broadcasted_iota(jnp.int32, (rows, t), 0) % t
    causal = lax.broadcasted_iota(jnp.int32, (rows, t), 1) <= pos
    masks = _half_masks(rows, SSD_HEADDIM)
    ys = []
    for g in range(SSD_GROUPS):
        b0 = 2 * SSD_WIDTH + g * SSD_STATE
        c0 = 2 * SSD_WIDTH + SSD_GROUPS * SSD_STATE + g * SSD_STATE
        bm = col_blk(zx_ref, b0, SSD_STATE).astype(BF16)
        cm = col_blk(zx_ref, c0, SSD_STATE).astype(BF16)
        cbm = stack([_dot_nt(seq(cm, sq), seq(bm, sq)) for sq in range(ns)])
        gs = slice(g * SSD_GW, (g + 1) * SSD_GW)
        h_prev = [hst[sq, g] for sq in range(ns)]
        y_g = stack([_dot(seq(cm, sq), h_prev[sq].astype(BF16)) for sq in range(ns)]) * eacs_e[:, gs]
        diag = []
        for pr in range(SSD_HPG // 2):
            ps = slice(g * SSD_GW + pr * LANES, g * SSD_GW + (pr + 1) * LANES)
            acc = None
            for q in range(2):
                j = g * SSD_HPG + pr * 2 + q
                a_row = stack([jnp.broadcast_to(acs_t[sq][j:j + 1, :], (t, t)) for sq in range(ns)])
                seg = acs[:, j:j + 1] - a_row
                m = (cbm * jnp.exp2(jnp.where(causal, seg, -jnp.inf))).astype(BF16)
                xm = xdt_b[:, ps] * masks[q]
                part = stack([_dot(seq(m, sq), seq(xm, sq)) for sq in range(ns)])
                acc = part if acc is None else acc + part
            diag.append(acc)
        ys.append(y_g + jnp.concatenate(diag, axis=1))
        for sq in range(ns):
            hst[sq, g] = h_prev[sq] * cdec[sq][:, gs] + _dot_tn(seq(bm, sq), seq(xw_b, sq)[:, gs])
    y = jnp.concatenate(ys, axis=1) + xs * dsk_ref[...]
    y = y * col_blk(zx_ref, 0, SSD_WIDTH)
    outs = []
    for g in range(SSD_GROUPS):
        gs = slice(g * SSD_GW, (g + 1) * SSD_GW)
        outs.append(_rms(y[:, gs], ng_ref[:, gs]))
    y_ref[...] = jnp.concatenate(outs, axis=1).astype(BF16).reshape(ns, t, SSD_WIDTH)


def _ssd(zx, gates, dtb, alog, dsk, ng, sel, *, layer, nseq, interpret=False):
    b, l, _ = zx.shape
    full = lambda a: _layer_spec(a, layer)
    tok = lambda width: pl.BlockSpec((nseq, CHUNK, width), lambda i, j: (i, j, 0))
    return pl.pallas_call(
        _ssd_kernel,
        grid=(b // nseq, l // CHUNK),
        in_specs=[tok(SSD_WIDTH + SSD_XBC), tok(LANES), full(dtb), full(alog), full(dsk), full(ng),
                  _whole_spec(sel)],
        out_specs=tok(SSD_WIDTH),
        out_shape=jax.ShapeDtypeStruct((b, l, SSD_WIDTH), BF16),
        scratch_shapes=[pltpu.VMEM((nseq, SSD_GROUPS, SSD_STATE, SSD_GW), F32)],
        compiler_params=pltpu.CompilerParams(
            dimension_semantics=("parallel", "arbitrary"), vmem_limit_bytes=VMEM_LIMIT),
        name="ssd",
        interpret=interpret,
    )(zx, gates, dtb, alog, dsk, ng, sel)


def _mlstm_kernel(qk_ref, v_ref, so_ref, gt_ref, bi_ref, bf_ref, sel_ref, y_ref, cst, nst, mst):
    ns = qk_ref.shape[0]
    t = CHUNK
    rows = ns * t

    @pl.when(pl.program_id(1) == 0)
    def _():
        cst[...] = jnp.zeros_like(cst)
        nst[...] = jnp.zeros_like(nst)
        mst[...] = jnp.zeros_like(mst)

    seq = lambda a, sq: a[sq * t:(sq + 1) * t]
    last = lambda a, sq: a[(sq + 1) * t - 1:(sq + 1) * t]
    stack = lambda parts: jnp.concatenate(parts, axis=0)
    per_seq = lambda vals: stack([jnp.broadcast_to(v, (t, v.shape[1])) for v in vals])

    gates = gt_ref[...].reshape(rows, LANES)
    log_i = gates + bi_ref[...]
    log_f = -_softplus(-(gates + bf_ref[...]))
    log_f = pltpu.roll(log_f, LANES - (GATE_F - GATE_I), axis=1)
    bcum = stack([_cumsum_rows(seq(log_f, sq)) for sq in range(ns)])
    gvec = log_i - bcum
    cmax = _cummax_rows_blocked(gvec, t)
    g_tot = [last(bcum, sq) for sq in range(ns)]
    gmax = [last(cmax, sq) for sq in range(ns)]
    m_prev = [mst[sq] for sq in range(ns)]
    w_inter = bcum + per_seq(m_prev)
    m_row = jnp.maximum(w_inter, bcum + cmax)
    alpha2 = (bcum - m_row) * LOG2E
    s_int = jnp.exp(w_inter - m_row)
    enm = jnp.exp(-m_row)
    e_end = jnp.exp(gvec - per_seq(gmax))
    gvec2_t = [seq(gvec, sq).T * LOG2E for sq in range(ns)]
    s_old, s_new = [], []
    for sq in range(ns):
        a_loc = g_tot[sq] + gmax[sq]
        m_new = jnp.maximum(g_tot[sq] + m_prev[sq], a_loc)
        s_old.append(jnp.exp(g_tot[sq] + m_prev[sq] - m_new))
        s_new.append(jnp.exp(a_loc - m_new))
        mst[sq] = m_new

    q = qk_ref[:, :, 0:MLSTM_QK].reshape(rows, MLSTM_QK)
    k = qk_ref[:, :, MLSTM_QK:].reshape(rows, MLSTM_QK)
    qb = q.astype(BF16)
    kb = k.astype(BF16)
    e_end_b = e_end.astype(BF16)
    nrow = lax.broadcasted_iota(jnp.int32, (MLSTM_QK, LANES), 0)
    nlane = lax.broadcasted_iota(jnp.int32, (MLSTM_QK, LANES), 1)
    own = (nlane >= GATE_I) & ((nlane - GATE_I) * MLSTM_QKDIM <= nrow) & (
        nrow < (nlane - GATE_I + 1) * MLSTM_QKDIM)
    qn = []
    for sq in range(ns):
        n_prev = nst[sq]
        qn.append(_dot(seq(qb, sq), n_prev.astype(BF16)))
        n_loc = jnp.where(own, _dot_tn(seq(kb, sq), seq(e_end_b, sq)), 0.0)
        nst[sq] = s_old[sq] * n_prev + s_new[sq] * n_loc
    sqn = s_int * stack(qn)
    kwb = (k * _expand_heads(e_end, sel_ref)).astype(BF16)

    pos = lax.broadcasted_iota(jnp.int32, (rows, t), 0) % t
    causal = lax.broadcasted_iota(jnp.int32, (rows, t), 1) <= pos
    masks = _half_masks(rows, MLSTM_QKDIM)
    srow = lax.broadcasted_iota(jnp.int32, (LANES, MLSTM_VDIM), 0)
    for pr in range(MLSTM_HEADS // 2):
        ps = slice(pr * LANES, (pr + 1) * LANES)
        c_prev = [cst[sq, pr] for sq in range(ns)]
        c_prev_b = [c.astype(BF16) for c in c_prev]
        upd = [[None, None] for _ in range(ns)]
        for hq in range(2):
            h = pr * 2 + hq
            ln = GATE_I + h
            hs = slice(h * MLSTM_VDIM, (h + 1) * MLSTM_VDIM)
            q_hb = qb[:, ps] * masks[hq]
            s = stack([_dot_nt(seq(q_hb, sq), seq(kb, sq)[:, ps]) for sq in range(ns)])
            g_row = stack([jnp.broadcast_to(gvec2_t[sq][ln:ln + 1, :], (t, t)) for sq in range(ns)])
            dm = jnp.where(causal, alpha2[:, ln:ln + 1] + g_row, -jnp.inf)
            qs = s * jnp.exp2(dm)
            qsb = qs.astype(BF16)
            num = (stack([_dot(seq(qsb, sq), v_ref[sq, :, hs]) for sq in range(ns)])
                   + s_int[:, ln:ln + 1]
                   * stack([_dot(seq(q_hb, sq), c_prev_b[sq]) for sq in range(ns)]))
            den = jnp.sum(qs, axis=1, keepdims=True) + sqn[:, ln:ln + 1]
            den = jnp.maximum(jnp.abs(den), enm[:, ln:ln + 1])
            out = so_ref[:, :, hs].reshape(rows, MLSTM_VDIM) * (num / den)
            y_ref[:, :, hs] = out.astype(BF16).reshape(ns, t, MLSTM_VDIM)
            for sq in range(ns):
                upd[sq][hq] = _dot_tn(seq(kwb, sq)[:, ps], v_ref[sq, :, hs])
        lo = GATE_I + 2 * pr
        for sq in range(ns):
            so = jnp.where(srow < MLSTM_QKDIM, s_old[sq][:, lo:lo + 1], s_old[sq][:, lo + 1:lo + 2])
            sn = jnp.where(srow < MLSTM_QKDIM, s_new[sq][:, lo:lo + 1], s_new[sq][:, lo + 1:lo + 2])
            cst[sq, pr] = so * c_prev[sq] + sn * jnp.where(srow < MLSTM_QKDIM, upd[sq][0], upd[sq][1])


def _mlstm(qk, v, so, gates, bi, bf, sel, *, layer, nseq, interpret=False):
    b, l, _ = qk.shape
    full = lambda a: _layer_spec(a, layer)
    tok = lambda width: pl.BlockSpec((nseq, CHUNK, width), lambda i, j: (i, j, 0))
    return pl.pallas_call(
        _mlstm_kernel,
        grid=(b // nseq, l // CHUNK),
        in_specs=[tok(2 * MLSTM_QK), tok(MLSTM_WIDTH), tok(MLSTM_WIDTH), tok(LANES),
                  full(bi), full(bf), _whole_spec(sel)],
        out_specs=tok(MLSTM_WIDTH),
        out_shape=jax.ShapeDtypeStruct((b, l, MLSTM_WIDTH), BF16),
        scratch_shapes=[pltpu.VMEM((nseq, MLSTM_HEADS // 2, LANES, MLSTM_VDIM), F32),
                        pltpu.VMEM((nseq, MLSTM_QK, LANES), F32),
                        pltpu.VMEM((nseq, 1, LANES), F32)],
        compiler_params=pltpu.CompilerParams(
            dimension_semantics=("parallel", "arbitrary"), vmem_limit_bytes=VMEM_LIMIT),
        name="mlstm",
        interpret=interpret,
    )(qk, v, so, gates, bi, bf, sel)


def _outffn_kernel(x_ref, ya_ref, yb_ref, yc_ref, wo_ref, g1_ref, g2_ref, wg_ref, wu_ref,
                   wd_ref, g3_ref, o_ref):
    a0 = S5_WIDTH
    a1 = S5_WIDTH + SSD_WIDTH
    mix = (_dot(ya_ref[0], wo_ref[0:a0, :]) + _dot(yb_ref[0], wo_ref[a0:a1, :])
           + _dot(yc_ref[0], wo_ref[a1:, :]))
    x1 = x_ref[0] + _rms(mix, g1_ref[...])
    h = _rms(x1, g2_ref[...]).astype(BF16)
    act = (_silu(_dot(h, wg_ref[...])) * _dot(h, wu_ref[...])).astype(BF16)
    ff = _dot(act, wd_ref[...])
    o_ref[0] = x1 + _rms(ff, g3_ref[...])


def _out_ffn(x, ya, yb, yc, wo, g1, g2, wg, wu, wd, g3, *, layer, tt, in_place, interpret=False):
    b, l, d = x.shape
    tok = lambda width: pl.BlockSpec((1, tt, width), lambda i, j: (i, j, 0))
    const = lambda a: _layer_spec(a, layer, pipeline_mode=pl.Buffered(1))
    return pl.pallas_call(
        _outffn_kernel,
        grid=(b, l // tt),
        in_specs=[tok(d), tok(S5_WIDTH), tok(SSD_WIDTH), tok(MLSTM_WIDTH), const(wo), const(g1),
                  const(g2), const(wg), const(wu), const(wd), const(g3)],
        out_specs=tok(d),
        out_shape=jax.ShapeDtypeStruct((b, l, d), F32),
        input_output_aliases={0: 0} if in_place else {},
        compiler_params=pltpu.CompilerParams(
            dimension_semantics=("parallel", "parallel"), vmem_limit_bytes=VMEM_LIMIT),
        name="out_ffn",
        interpret=interpret,
    )(x, ya, yb, yc, wo, g1, g2, wg, wu, wd, g3)


def _lane_row(vals, offset):
    return jnp.zeros((1, LANES), F32).at[0, offset:offset + vals.shape[0]].set(vals)


def _pack_w_in(w_in):
    offs = [0]
    for s in IN_SPLIT_SIZES:
        offs.append(offs[-1] + s)
    seg = lambda i: w_in[..., offs[i]:offs[i + 1]]
    u, z, xbc, dt, qk, v, gi, gf, o = (seg(i) for i in range(9))
    pad = jnp.zeros(w_in.shape[:-1] + (LANES - SSD_HEADS - 2 * MLSTM_HEADS,), w_in.dtype)
    return jnp.concatenate([u, z, xbc, qk, v, o, dt, gi, gf, pad], axis=-1).astype(BF16)


def _layer(x, p, consts, layer, *, tt_in, tt_out, tc, nseq_ssd, nseq_mlstm, interpret):
    u, zx, qk, v, so, gates = _in_proj(x, p["norm_pre_mix"], p["w_in"], p["conv_w"], p["conv_b"],
                                       consts["conv_scale"], layer=layer, tt=tt_in,
                                       interpret=interpret)
    ya = _s5(u, p["s5_bblk"], p["s5_cblk"], p["s5_are"], p["s5_aim"],
             p["s5_d"], p["s5_w_glu"], p["s5_b_glu"], layer=layer, tc=tc, interpret=interpret)
    yb = _ssd(zx, gates, p["ssd_dtb"], p["ssd_alog"], p["ssd_dsk"], p["ssd_norm"],
              consts["ssd_sel"], layer=layer, nseq=nseq_ssd, interpret=interpret)
    yc = _mlstm(qk, v, so, gates, p["mlstm_bi"], p["mlstm_bf"], consts["mlstm_sel"],
                layer=layer, nseq=nseq_mlstm, interpret=interpret)
    return _out_ffn(x, ya, yb, yc, p["w_out"], p["norm_post_mix"],
                    p["norm_pre_ffn"], p["w_gate"], p["w_up"], p["w_down"], p["norm_post_ffn"],
                    layer=layer, tt=tt_out, in_place=layer > 0, interpret=interpret)


def _prepare_params(norm_pre_mix, w_in, s5_lambda_re, s5_lambda_im, s5_log_step, s5_b_re, s5_b_im,
                    s5_c_re, s5_c_im, s5_d, s5_w_glu, s5_b_glu, ssd_conv_w, ssd_conv_b,
                    ssd_dt_bias, ssd_a_log, ssd_d, ssd_norm, mlstm_conv_w, mlstm_conv_b, mlstm_b_i,
                    mlstm_b_f, w_out, norm_post_mix, norm_pre_ffn, w_gate, w_up, w_down,
                    norm_post_ffn):
    depth = w_in.shape[0]
    f32 = lambda a: a.astype(F32)
    row = lambda a: f32(a).reshape(depth, 1, -1)
    lanes = lambda a, off: jax.vmap(lambda v: _lane_row(v, off))(f32(a))
    bblk, cblk, a_re, a_im = jax.vmap(_s5_prepare)(
        f32(s5_lambda_re), f32(s5_lambda_im), f32(s5_log_step), f32(s5_b_re), f32(s5_b_im),
        f32(s5_c_re), f32(s5_c_im))
    return dict(
        norm_pre_mix=row(norm_pre_mix), w_in=_pack_w_in(w_in),
        conv_w=f32(jnp.concatenate([ssd_conv_w, mlstm_conv_w], axis=-1)),
        conv_b=row(jnp.concatenate([ssd_conv_b, mlstm_conv_b], axis=-1)),
        s5_bblk=bblk, s5_cblk=cblk, s5_are=a_re, s5_aim=a_im, s5_d=row(s5_d),
        s5_w_glu=s5_w_glu.astype(BF16), s5_b_glu=row(s5_b_glu),
        ssd_dtb=lanes(ssd_dt_bias, GATE_DT), ssd_alog=lanes(ssd_a_log, GATE_DT),
        ssd_dsk=row(jnp.repeat(ssd_d, SSD_HEADDIM, axis=-1)), ssd_norm=row(ssd_norm),
        mlstm_bi=lanes(mlstm_b_i, GATE_I), mlstm_bf=lanes(mlstm_b_f, GATE_F),
        w_out=w_out.astype(BF16), norm_post_mix=row(norm_post_mix),
        norm_pre_ffn=row(norm_pre_ffn), w_gate=w_gate.astype(BF16), w_up=w_up.astype(BF16),
        w_down=w_down.astype(BF16), norm_post_ffn=row(norm_post_ffn))


def _forward(x, *params, tt_in=512, tt_out=512, tc=64, nseq_ssd=4, nseq_mlstm=8, interpret=False):
    conv_scale = jnp.concatenate(
        [jnp.ones((SSD_XBC + MLSTM_QK,), F32), jnp.full((MLSTM_QK,), MLSTM_QKDIM ** -0.5, F32)])
    consts = dict(ssd_sel=_expand_matrix(GATE_DT, SSD_HEADS, SSD_HEADDIM),
                  mlstm_sel=_expand_matrix(GATE_I, MLSTM_HEADS, MLSTM_QKDIM),
                  conv_scale=conv_scale.reshape(1, CONV_W))
    stacked = _prepare_params(*params)
    for layer in range(stacked["w_in"].shape[0]):
        x = _layer(x, stacked, consts, layer, tt_in=tt_in, tt_out=tt_out, tc=tc,
                   nseq_ssd=nseq_ssd, nseq_mlstm=nseq_mlstm, interpret=interpret)
    return x


def kernel(x, norm_pre_mix, w_in, s5_lambda_re, s5_lambda_im, s5_log_step, s5_b_re, s5_b_im, s5_c_re, s5_c_im, s5_d, s5_w_glu, s5_b_glu, ssd_conv_w, ssd_conv_b, ssd_dt_bias, ssd_a_log, ssd_d, ssd_norm, mlstm_conv_w, mlstm_conv_b, mlstm_b_i, mlstm_b_f, w_out, norm_post_mix, norm_pre_ffn, w_gate, w_up, w_down, norm_post_ffn):
    return _forward(x, norm_pre_mix, w_in, s5_lambda_re, s5_lambda_im, s5_log_step, s5_b_re,
                    s5_b_im, s5_c_re, s5_c_im, s5_d, s5_w_glu, s5_b_glu, ssd_conv_w, ssd_conv_b,
                    ssd_dt_bias, ssd_a_log, ssd_d, ssd_norm, mlstm_conv_w, mlstm_conv_b, mlstm_b_i,
                    mlstm_b_f, w_out, norm_post_mix, norm_pre_ffn, w_gate, w_up, w_down,
                    norm_post_ffn)
```
